```python
import jax, jax.numpy as jnp
from jax import lax
import numpy as np

D_MODEL = 2048
BATCH = 2
SEQ = 16384
DEPTH = 1

N_MEM = 256
LRU_WIDTH = D_MODEL
LRU_BLOCKS = 16
LRU_BLOCK_DIM = LRU_WIDTH // LRU_BLOCKS
CONV_WIDTH = 4
LRU_C = 8.0
HEAD_DIM = 128
N_Q_HEADS = 16
N_KV_HEADS = 4
Q_GROUP = N_Q_HEADS // N_KV_HEADS
WINDOW = 128
BLOCK = 128
N_X_HEADS = 4
X_HEAD_DIM = D_MODEL // N_X_HEADS
D_FF = 4 * D_MODEL
N_BRANCH = 3
ROPE_THETA = 10000.0
EPS = 1e-6
SPLITS = (LRU_WIDTH, LRU_WIDTH, N_Q_HEADS * HEAD_DIM, N_KV_HEADS * HEAD_DIM,
          N_KV_HEADS * HEAD_DIM, N_X_HEADS * X_HEAD_DIM, N_BRANCH * D_MODEL)

kernel_name = 'hybrid_rglru_swa_memxattn_block'


def rmsnorm(x, g):
    xf = x.astype(jnp.float32)
    var = jnp.mean(xf * xf, axis=-1, keepdims=True)
    return (xf * lax.rsqrt(var + EPS) * g.astype(jnp.float32)).astype(x.dtype)


def rope(t, positions):
    half = t.shape[-1] // 2
    freqs = ROPE_THETA ** (-jnp.arange(half, dtype=jnp.float32) / half)
    ang = positions.astype(jnp.float32)[..., None] * freqs
    cos = jnp.cos(ang)[:, :, None, :]
    sin = jnp.sin(ang)[:, :, None, :]
    tf = t.astype(jnp.float32)
    t1, t2 = tf[..., :half], tf[..., half:]
    return jnp.concatenate([t1 * cos - t2 * sin, t2 * cos + t1 * sin], axis=-1).astype(t.dtype)


def _lin_combine(c1, c2):
    a1, b1 = c1
    a2, b2 = c2
    return a1 * a2, a2 * b1 + b2


def rglru_direction(xc, w_r, b_r, w_i, b_i, lam, reverse):
    B, S, C = xc.shape
    xb = xc.reshape(B, S, LRU_BLOCKS, LRU_BLOCK_DIM)
    r = jax.nn.sigmoid((jnp.einsum('bshi,hij->bshj', xb, w_r) + b_r).reshape(B, S, C).astype(jnp.float32))
    i = jax.nn.sigmoid((jnp.einsum('bshi,hij->bshj', xb, w_i) + b_i).reshape(B, S, C).astype(jnp.float32))
    log_a = -LRU_C * r * jax.nn.softplus(-lam.astype(jnp.float32))
    a = jnp.exp(log_a)
    u = jnp.sqrt(-jnp.expm1(2.0 * log_a)) * (i * xc.astype(jnp.float32))
    if reverse:
        a, u = jnp.flip(a, axis=1), jnp.flip(u, axis=1)
    _, h = lax.associative_scan(_lin_combine, (a, u), axis=1)
    if reverse:
        h = jnp.flip(h, axis=1)
    return h


def local_attention(q, k, v, sink):
    B, S, _, D = q.shape
    nb = S // BLOCK
    qb = q.reshape(B, nb, BLOCK, N_KV_HEADS, Q_GROUP, D)

    def band(t):
        tp = jnp.pad(t, ((0, 0), (BLOCK, BLOCK), (0, 0), (0, 0))).reshape(B, nb + 2, BLOCK, N_KV_HEADS, D)
        return jnp.concatenate([tp[:, :-2], tp[:, 1:-1], tp[:, 2:]], axis=2)

    kb, vb = band(k), band(v)
    s = jnp.einsum('bnqkgd,bnskd->bnkgqs', qb, kb).astype(jnp.float32) * (D ** -0.5)
    blk = jnp.arange(nb, dtype=jnp.int32)[:, None] * BLOCK
    qpos = blk + jnp.arange(BLOCK, dtype=jnp.int32)[None, :]
    kpos = blk - BLOCK + jnp.arange(3 * BLOCK, dtype=jnp.int32)[None, :]
    rel = kpos[:, None, :] - qpos[:, :, None]
    valid = (jnp.abs(rel) <= WINDOW) & (kpos[:, None, :] >= 0) & (kpos[:, None, :] < S)
    s = jnp.where(valid[None, :, None, None], s, -jnp.inf)
    sink_f = sink.astype(jnp.float32).reshape(N_KV_HEADS, Q_GROUP)[None, None, :, :, None, None]
    m = jnp.maximum(jnp.max(s, axis=-1, keepdims=True), sink_f)
    p = jnp.exp(s - m)
    denom = jnp.sum(p, axis=-1, keepdims=True) + jnp.exp(sink_f - m)
    o = jnp.einsum('bnkgqs,bnskd->bnqkgd', (p / denom).astype(v.dtype), vb)
    return o.reshape(B, S, N_Q_HEADS * D)


def memory_attention(q, mk, mv):
    B, S = q.shape[0], q.shape[1]
    s = jnp.einsum('bshd,bmhd->bhsm', q, mk).astype(jnp.float32) * (X_HEAD_DIM ** -0.5)
    p = jax.nn.softmax(s, axis=-1).astype(mv.dtype)
    return jnp.einsum('bhsm,bmhd->bshd', p, mv).reshape(B, S, N_X_HEADS * X_HEAD_DIM)


def hybrid_layer(x, mem, positions, norm_mix_pre, norm_mix_post, norm_mem, w_in, b_gate,
                 conv_w, conv_b, wr_f, br_f, wi_f, bi_f, lam_f, wr_b, br_b, wi_b, bi_b, lam_b,
                 attn_sink, w_mem_kv, w_br_lru, w_br_attn, w_br_mem, w_out,
                 norm_mlp_pre, norm_mlp_post, w_up, w_down):
    B, S, _ = x.shape
    h = rmsnorm(x, norm_mix_pre)
    proj = h @ w_in
    idx = [int(c) for c in np.cumsum(SPLITS)[:-1]]
    xr, gr, q, k, v, qm, gl = jnp.split(proj, idx, axis=-1)

    xc = lax.conv_general_dilated(xr, conv_w[:, None, :], window_strides=(1,),
                                  padding=[(1, CONV_WIDTH - 2)],
                                  dimension_numbers=('NWC', 'WIO', 'NWC'),
                                  feature_group_count=LRU_WIDTH) + conv_b
    h_lru = (rglru_direction(xc, wr_f, br_f, wi_f, bi_f, lam_f, False)
             + rglru_direction(xc, wr_b, br_b, wi_b, bi_b, lam_b, True))
    y_lru = (h_lru * jax.nn.gelu(gr.astype(jnp.float32))).astype(x.dtype)

    q = rope(q.reshape(B, S, N_Q_HEADS, HEAD_DIM), positions)
    k = rope(k.reshape(B, S, N_KV_HEADS, HEAD_DIM), positions)
    v = v.reshape(B, S, N_KV_HEADS, HEAD_DIM)
    y_attn = local_attention(q, k, v, attn_sink)

    mn = rmsnorm(mem, norm_mem)
    mk, mv = jnp.split(mn @ w_mem_kv, 2, axis=-1)
    M = mem.shape[1]
    y_mem = memory_attention(qm.reshape(B, S, N_X_HEADS, X_HEAD_DIM),
                             mk.reshape(B, M, N_X_HEADS, X_HEAD_DIM),
                             mv.reshape(B, M, N_X_HEADS, X_HEAD_DIM))

    gates = jax.nn.sigmoid((gl + b_gate).astype(jnp.float32)).reshape(B, S, N_BRANCH, D_MODEL)
    merged = (gates[:, :, 0] * (y_lru @ w_br_lru).astype(jnp.float32)
              + gates[:, :, 1] * (y_attn @ w_br_attn).astype(jnp.float32)
              + gates[:, :, 2] * (y_mem @ w_br_mem).astype(jnp.float32)).astype(x.dtype)
    x = x + rmsnorm(merged @ w_out, norm_mix_post)

    hm = rmsnorm(x, norm_mlp_pre)
    u = jnp.square(jax.nn.relu(hm @ w_up))
    x = x + rmsnorm(u @ w_down, norm_mlp_post)
    return x


def setup_inputs(seed: int = 0) -> dict:
    key = jax.random.key(seed)
    ks = jax.random.split(key, 40)
    L = DEPTH
    f32 = jnp.float32

    def nrm(k, shape, scale):
        return jax.random.normal(k, shape, f32) * scale

    def gain(k, shape):
        return 1.0 + 0.05 * jax.random.normal(k, shape, f32)

    def lam_init(k):
        a0 = jax.random.uniform(k, (L, LRU_WIDTH), f32, minval=0.9, maxval=0.999)
        return jnp.log(a0) - jnp.log1p(-a0)

    n_in = sum(SPLITS)
    bd = LRU_BLOCK_DIM
    return {
        'x': nrm(ks[0], (BATCH, SEQ, D_MODEL), 1.0),
        'mem': nrm(ks[1], (BATCH, N_MEM, D_MODEL), 1.0),
        'positions': (jnp.arange(SEQ, dtype=jnp.int32)[None, :]
                      + jax.random.randint(ks[2], (BATCH, 1), 0, 1024, dtype=jnp.int32)),
        'norm_mix_pre': gain(ks[3], (L, D_MODEL)),
        'norm_mix_post': gain(ks[4], (L, D_MODEL)),
        'norm_mem': gain(ks[5], (L, D_MODEL)),
        'w_in': nrm(ks[6], (L, D_MODEL, n_in), D_MODEL ** -0.5),
        'b_gate': nrm(ks[7], (L, N_BRANCH * D_MODEL), 0.1),
        'conv_w': nrm(ks[8], (L, CONV_WIDTH, LRU_WIDTH), CONV_WIDTH ** -0.5),
        'conv_b': nrm(ks[9], (L, LRU_WIDTH), 0.02),
        'wr_f': nrm(ks[10], (L, LRU_BLOCKS, bd, bd), bd ** -0.5),
        'br_f': nrm(ks[11], (L, LRU_BLOCKS, bd), 0.02),
        'wi_f': nrm(ks[12], (L, LRU_BLOCKS, bd, bd), bd ** -0.5),
        'bi_f': nrm(ks[13], (L, LRU_BLOCKS, bd), 0.02),
        'lam_f': lam_init(ks[14]),
        'wr_b': nrm(ks[15], (L, LRU_BLOCKS, bd, bd), bd ** -0.5),
        'br_b': nrm(ks[16], (L, LRU_BLOCKS, bd), 0.02),
        'wi_b': nrm(ks[17], (L, LRU_BLOCKS, bd, bd), bd ** -0.5),
        'bi_b': nrm(ks[18], (L, LRU_BLOCKS, bd), 0.02),
        'lam_b': lam_init(ks[19]),
        'attn_sink': nrm(ks[20], (L, N_Q_HEADS), 0.5),
        'w_mem_kv': nrm(ks[21], (L, D_MODEL, 2 * N_X_HEADS * X_HEAD_DIM), D_MODEL ** -0.5),
        'w_br_lru': nrm(ks[22], (L, LRU_WIDTH, D_MODEL), LRU_WIDTH ** -0.5),
        'w_br_attn': nrm(ks[23], (L, N_Q_HEADS * HEAD_DIM, D_MODEL), (N_Q_HEADS * HEAD_DIM) ** -0.5),
        'w_br_mem': nrm(ks[24], (L, N_X_HEADS * X_HEAD_DIM, D_MODEL), (N_X_HEADS * X_HEAD_DIM) ** -0.5),
        'w_out': nrm(ks[25], (L, D_MODEL, D_MODEL), D_MODEL ** -0.5),
        'norm_mlp_pre': gain(ks[26], (L, D_MODEL)),
        'norm_mlp_post': gain(ks[27], (L, D_MODEL)),
        'w_up': nrm(ks[28], (L, D_MODEL, D_FF), D_MODEL ** -0.5),
        'w_down': nrm(ks[29], (L, D_FF, D_MODEL), D_FF ** -0.5),
    }


def reference(x, mem, positions, norm_mix_pre, norm_mix_post, norm_mem, w_in, b_gate,
              conv_w, conv_b, wr_f, br_f, wi_f, bi_f, lam_f, wr_b, br_b, wi_b, bi_b, lam_b,
              attn_sink, w_mem_kv, w_br_lru, w_br_attn, w_br_mem, w_out,
              norm_mlp_pre, norm_mlp_post, w_up, w_down):
    for l in range(DEPTH):
        x = hybrid_layer(x, mem, positions, norm_mix_pre[l], norm_mix_post[l], norm_mem[l],
                         w_in[l], b_gate[l], conv_w[l], conv_b[l],
                         wr_f[l], br_f[l], wi_f[l], bi_f[l], lam_f[l],
                         wr_b[l], br_b[l], wi_b[l], bi_b[l], lam_b[l],
                         attn_sink[l], w_mem_kv[l], w_br_lru[l], w_br_attn[l], w_br_mem[l],
                         w_out[l], norm_mlp_pre[l], norm_mlp_post[l], w_up[l], w_down[l])
    return x
```

```python
import functools
import math

import jax
import jax.numpy as jnp
from jax import lax
from jax.experimental import pallas as pl
from jax.experimental.pallas import tpu as pltpu

LRU_BLOCKS = 16
LRU_C = 8.0
HEAD_DIM = 128
N_Q_HEADS = 16
N_KV_HEADS = 4
Q_GROUP = N_Q_HEADS // N_KV_HEADS
WINDOW = 128
N_X_HEADS = 4
N_BRANCH = 3
ROPE_THETA = 10000.0
EPS = 1e-6
CONV_WIDTH = 4

V7X_LANES = 128
V7X_SUBLANES = 8
V7X_BF16_SUBLANES = 16
V7X_VMEM_BYTES = 64 * 1024 * 1024

F32 = jnp.float32
BF16 = jnp.bfloat16
MASK_VALUE = -1e30


def _vmem_limit(nbytes):
    return int(min(nbytes, V7X_VMEM_BYTES - 4 * 1024 * 1024))


def _sigmoid(x):
    return 0.5 * jnp.tanh(0.5 * x) + 0.5


def _rms_scale(x):
    var = jnp.mean(x * x, axis=-1, keepdims=True)
    return x * lax.rsqrt(var + EPS)


def _norm_matmul_body(x_ref, g_ref, w_ref, o_ref, h_ref):
    @pl.when(pl.program_id(1) == 0)
    def _():
        h_ref[...] = (_rms_scale(x_ref[...]) * g_ref[...]).astype(h_ref.dtype)

    o_ref[...] = jnp.dot(h_ref[...], w_ref[...], preferred_element_type=F32).astype(o_ref.dtype)


def _norm_matmul(x, g, w, *, tm, tn, out_dtype, name):
    m, d = x.shape
    n = w.shape[1]
    est = 2 * tm * d * 4 + 2 * d * tn * 2 + 2 * tm * tn * 2 + tm * d * 2 + 3 * tm * tn * 4 + tm * d * 4
    return pl.pallas_call(
        _norm_matmul_body,
        grid=(m // tm, n // tn),
        in_specs=[
            pl.BlockSpec((tm, d), lambda i, j: (i, 0)),
            pl.BlockSpec((1, d), lambda i, j: (0, 0)),
            pl.BlockSpec((d, tn), lambda i, j: (0, j)),
        ],
        out_specs=pl.BlockSpec((tm, tn), lambda i, j: (i, j)),
        out_shape=jax.ShapeDtypeStruct((m, n), out_dtype),
        scratch_shapes=[pltpu.VMEM((tm, d), BF16)],
        compiler_params=pltpu.CompilerParams(
            dimension_semantics=("parallel", "arbitrary"), vmem_limit_bytes=_vmem_limit(est)),
        name=name,
    )(x, g.reshape(1, d), w)


def _rope_table_body(pos_ref, freq_ref, sign_ref, cos_ref, sin_ref):
    ang = pos_ref[...].astype(F32) * freq_ref[...]
    cos_ref[...] = jnp.cos(ang)
    sin_ref[...] = jnp.sin(ang) * sign_ref[...]


def _rope_tables(positions, *, tr):
    t = positions.size
    half = HEAD_DIM // 2
    freqs = ROPE_THETA ** (-jnp.arange(half, dtype=F32) / half)
    freq2 = jnp.concatenate([freqs, freqs]).reshape(1, HEAD_DIM)
    sign = jnp.concatenate([-jnp.ones((half,), F32), jnp.ones((half,), F32)]).reshape(1, HEAD_DIM)
    tab = jax.ShapeDtypeStruct((t, HEAD_DIM), F32)
    return pl.pallas_call(
        _rope_table_body,
        grid=(t // tr,),
        in_specs=[
            pl.BlockSpec((tr, 1), lambda i: (i, 0)),
            pl.BlockSpec((1, HEAD_DIM), lambda i: (0, 0)),
            pl.BlockSpec((1, HEAD_DIM), lambda i: (0, 0)),
        ],
        out_specs=[pl.BlockSpec((tr, HEAD_DIM), lambda i: (i, 0))] * 2,
        out_shape=[tab, tab],
        compiler_params=pltpu.CompilerParams(dimension_semantics=("parallel",)),
        name="rope_tables",
    )(positions.reshape(t, 1), freq2, sign)


def _rope_body(x_ref, cos_ref, sin_ref, o_ref):
    cos = cos_ref[...]
    sin = sin_ref[...]
    for h in range(x_ref.shape[1] // HEAD_DIM):
        sl = slice(h * HEAD_DIM, (h + 1) * HEAD_DIM)
        t = x_ref[:, sl].astype(F32)
        o_ref[:, sl] = (t * cos + pltpu.roll(t, HEAD_DIM // 2, 1) * sin).astype(o_ref.dtype)


def _rope_qk(proj, cos, sin, *, tr, col0, ncols, wblk):
    t = proj.shape[0]
    b0 = col0 // wblk
    return pl.pallas_call(
        _rope_body,
        grid=(t // tr, ncols // wblk),
        in_specs=[
            pl.BlockSpec((tr, wblk), lambda i, j: (i, b0 + j)),
            pl.BlockSpec((tr, HEAD_DIM), lambda i, j: (i, 0)),
            pl.BlockSpec((tr, HEAD_DIM), lambda i, j: (i, 0)),
        ],
        out_specs=pl.BlockSpec((tr, wblk), lambda i, j: (i, j)),
        out_shape=jax.ShapeDtypeStruct((t, ncols), BF16),
        compiler_params=pltpu.CompilerParams(dimension_semantics=("parallel", "arbitrary")),
        name="rope_qk",
    )(proj, cos, sin)


def _gelu_tanh(x):
    return 0.5 * x * (1.0 + jnp.tanh(math.sqrt(2.0 / math.pi) * (x + 0.044715 * (x * x * x))))


def _scan_tile(a, u, carry, masks, reverse):
    nv = a.shape[0] // V7X_SUBLANES
    outs = [None] * nv
    order = range(nv - 1, -1, -1) if reverse else range(nv)
    for q in order:
        av = a[q * V7X_SUBLANES:(q + 1) * V7X_SUBLANES]
        uv = u[q * V7X_SUBLANES:(q + 1) * V7X_SUBLANES]
        for d, m in zip((1, 2, 4), masks):
            sh = V7X_SUBLANES - d if reverse else d
            us = jnp.where(m, 0.0, pltpu.roll(uv, sh, 0))
            as_ = jnp.where(m, 1.0, pltpu.roll(av, sh, 0))
            uv = uv + av * us
            av = av * as_
        hv = uv + av * carry
        carry = hv[0:1] if reverse else hv[V7X_SUBLANES - 1:V7X_SUBLANES]
        outs[q] = hv
    return jnp.concatenate(outs, axis=0), carry


def _lru_body(xr_ref, gr_ref, wf_ref, wb_ref, bf_ref, bb_ref, lamf_ref, lamb_ref, cw_ref, cb_ref,
              y_ref, h_ref, *, ts):
    seq = xr_ref.shape[0]
    n = seq // ts
    halo = V7X_BF16_SUBLANES
    cw = cw_ref[...]
    cb = cb_ref[...]
    row = lax.broadcasted_iota(jnp.int32, (V7X_SUBLANES, V7X_LANES), 0)
    masks_f = tuple(row < d for d in (1, 2, 4))
    masks_b = tuple(row >= V7X_SUBLANES - d for d in (1, 2, 4))

    def neg_c_softplus(lam_ref):
        z = -lam_ref[...]
        sp = jnp.maximum(z, 0.0) + jnp.log1p(jnp.exp(-jnp.abs(z)))
        return -LRU_C * sp

    def conv_tile(j):
        t0 = pl.multiple_of(j * ts, ts)
        cur = xr_ref[pl.ds(t0, ts), :].astype(F32)
        p0 = pl.multiple_of(jnp.maximum(t0 - halo, 0), halo)
        n0 = pl.multiple_of(jnp.minimum(t0 + ts, seq - halo), halo)
        prev = jnp.where(j > 0, xr_ref[pl.ds(p0, halo), :].astype(F32), 0.0)
        nxt = jnp.where(j < n - 1, xr_ref[pl.ds(n0, halo), :].astype(F32), 0.0)
        ext = jnp.concatenate([prev, cur, nxt], axis=0)
        xc = (cw[0:1] * ext[halo - 1:halo - 1 + ts] + cw[1:2] * cur
              + cw[2:3] * ext[halo + 1:halo + 1 + ts] + cw[3:4] * ext[halo + 2:halo + 2 + ts])
        return xc + cb, t0

    def gates(xc, w_ref, b_ref, c):
        g = jnp.dot(xc.astype(BF16), w_ref[...], preferred_element_type=F32) + b_ref[...]
        r = _sigmoid(g[:, :V7X_LANES])
        i = _sigmoid(g[:, V7X_LANES:])
        log_a = c * r
        a = jnp.exp(log_a)
        one_minus_a2 = -jnp.tanh(log_a) * (a * a + 1.0)
        u = jnp.sqrt(one_minus_a2) * (i * xc)
        return a, u

    c_f = neg_c_softplus(lamf_ref)
    c_b = neg_c_softplus(lamb_ref)

    def fwd_step(j, carry):
        xc, t0 = conv_tile(j)
        a, u = gates(xc, wf_ref, bf_ref, c_f)
        h, carry = _scan_tile(a, u, carry, masks_f, reverse=False)
        h_ref[pl.ds(t0, ts), :] = h
        return carry

    lax.fori_loop(0, n, fwd_step, jnp.zeros((1, V7X_LANES), F32))

    def bwd_step(jj, carry):
        j = n - 1 - jj
        xc, t0 = conv_tile(j)
        a, u = gates(xc, wb_ref, bb_ref, c_b)
        h, carry = _scan_tile(a, u, carry, masks_b, reverse=True)
        gate = _gelu_tanh(gr_ref[pl.ds(t0, ts), :].astype(F32))
        y_ref[pl.ds(t0, ts), :] = ((h_ref[pl.ds(t0, ts), :] + h) * gate).astype(y_ref.dtype)
        return carry

    lax.fori_loop(0, n, bwd_step, jnp.zeros((1, V7X_LANES), F32))


def _lru_branch(proj3, wf, wb, bf, bb, lam_f, lam_b, conv_w, conv_b, *, ts):
    b, s, _ = proj3.shape
    c = LRU_BLOCKS
    bd = V7X_LANES
    gr0 = (c * bd) // bd
    est = 4 * s * bd * 2 + 2 * s * bd * 2 + s * bd * 4 + 8 * 1024 * 1024
    blk = lambda shape: pl.BlockSpec((None,) + shape, lambda bi, ci: (ci,) + (0,) * len(shape))
    return pl.pallas_call(
        functools.partial(_lru_body, ts=ts),
        grid=(b, c),
        in_specs=[
            pl.BlockSpec((None, s, bd), lambda bi, ci: (bi, 0, ci)),
            pl.BlockSpec((None, s, bd), lambda bi, ci: (bi, 0, gr0 + ci)),
            blk((bd, 2 * bd)), blk((bd, 2 * bd)),
            blk((1, 2 * bd)), blk((1, 2 * bd)),
            blk((1, bd)), blk((1, bd)),
            blk((CONV_WIDTH, bd)), blk((1, bd)),
        ],
        out_specs=pl.BlockSpec((None, s, bd), lambda bi, ci: (bi, 0, ci)),
        out_shape=jax.ShapeDtypeStruct((b, s, c * bd), BF16),
        scratch_shapes=[pltpu.VMEM((s, bd), F32)],
        compiler_params=pltpu.CompilerParams(
            dimension_semantics=("parallel", "parallel"), vmem_limit_bytes=_vmem_limit(est)),
        name="rglru",
    )(proj3, proj3, wf, wb, bf, bb, lam_f, lam_b, conv_w, conv_b)


def _win_attn_body(sink_ref, q_ref, kp_ref, kc_ref, kn_ref, vp_ref, vc_ref, vn_ref, o_ref, *, seq):
    g = pl.program_id(1)
    i = pl.program_id(2)
    tq = q_ref.shape[0]
    nkeys = tq + 2 * WINDOW
    k = jnp.concatenate([kp_ref[...], kc_ref[...], kn_ref[...]], axis=0)
    v = jnp.concatenate([vp_ref[...], vc_ref[...], vn_ref[...]], axis=0)
    rowi = lax.broadcasted_iota(jnp.int32, (tq, nkeys), 0)
    coli = lax.broadcasted_iota(jnp.int32, (tq, nkeys), 1)
    kpos = coli + (i * tq - WINDOW)
    valid = (coli >= rowi) & (coli <= rowi + 2 * WINDOW) & (kpos >= 0) & (kpos < seq)
    scale = HEAD_DIM ** -0.5
    for r in range(Q_GROUP):
        sl = slice(r * HEAD_DIM, (r + 1) * HEAD_DIM)
        s = lax.dot_general(q_ref[:, sl], k, (((1,), (1,)), ((), ())), preferred_element_type=F32)
        s = jnp.where(valid, s * scale, MASK_VALUE)
        sink = sink_ref[g * Q_GROUP + r]
        m = jnp.maximum(jnp.max(s, axis=-1, keepdims=True), sink)
        p = jnp.exp(s - m)
        denom = jnp.sum(p, axis=-1, keepdims=True) + jnp.exp(sink - m)
        o = jnp.dot(p.astype(BF16), v, preferred_element_type=F32)
        o_ref[:, sl] = (o / denom).astype(o_ref.dtype)


def _window_attention(qk3, proj3, sink, *, tq, k_blk0, v_blk0):
    b, s, _ = qk3.shape
    w = WINDOW
    per = tq // w
    nblk = s // w
    gw = Q_GROUP * HEAD_DIM
    prev = lambda blk0: pl.BlockSpec(
        (None, w, HEAD_DIM), lambda bi, gi, i: (bi, jnp.maximum(i * per - 1, 0), blk0 + gi))
    cur = lambda blk0: pl.BlockSpec((None, tq, HEAD_DIM), lambda bi, gi, i: (bi, i, blk0 + gi))
    nxt = lambda blk0: pl.BlockSpec(
        (None, w, HEAD_DIM), lambda bi, gi, i: (bi, jnp.minimum((i + 1) * per, nblk - 1), blk0 + gi))
    return pl.pallas_call(
        functools.partial(_win_attn_body, seq=s),
        grid=(b, N_KV_HEADS, s // tq),
        in_specs=[
            pl.BlockSpec(memory_space=pltpu.SMEM),
            pl.BlockSpec((None, tq, gw), lambda bi, gi, i: (bi, i, gi)),
            prev(k_blk0[0]), cur(k_blk0[0]), nxt(k_blk0[0]),
            prev(v_blk0), cur(v_blk0), nxt(v_blk0),
        ],
        out_specs=pl.BlockSpec((None, tq, gw), lambda bi, gi, i: (bi, i, gi)),
        out_shape=jax.ShapeDtypeStruct((b, s, N_Q_HEADS * HEAD_DIM), BF16),
        compiler_params=pltpu.CompilerParams(dimension_semantics=("parallel", "parallel", "parallel")),
        name="window_attention",
    )(sink, qk3, qk3, qk3, qk3, proj3, proj3, proj3)


def _mem_attn_body(q_ref, mk_ref, mv_ref, o_ref):
    xd = q_ref.shape[1]
    s = lax.dot_general(q_ref[...], mk_ref[...], (((1,), (1,)), ((), ())), preferred_element_type=F32)
    s = s * (xd ** -0.5)
    m = jnp.max(s, axis=-1, keepdims=True)
    p = jnp.exp(s - m)
    denom = jnp.sum(p, axis=-1, keepdims=True)
    o = jnp.dot(p.astype(BF16), mv_ref[...], preferred_element_type=F32)
    o_ref[...] = (o / denom).astype(o_ref.dtype)


def _memory_attention(proj3, mkv3, *, tq, q_blk0):
    b, s, _ = proj3.shape
    m = mkv3.shape[1]
    xd = mkv3.shape[2] // (2 * N_X_HEADS)
    return pl.pallas_call(
        _mem_attn_body,
        grid=(b, s // tq, N_X_HEADS),
        in_specs=[
            pl.BlockSpec((None, tq, xd), lambda bi, i, h: (bi, i, q_blk0 + h)),
            pl.BlockSpec((None, m, xd), lambda bi, i, h: (bi, 0, h)),
            pl.BlockSpec((None, m, xd), lambda bi, i, h: (bi, 0, N_X_HEADS + h)),
        ],
        out_specs=pl.BlockSpec((None, tq, xd), lambda bi, i, h: (bi, i, h)),
        out_shape=jax.ShapeDtypeStruct((b, s, N_X_HEADS * xd), BF16),
        compiler_params=pltpu.CompilerParams(dimension_semantics=("parallel", "parallel", "parallel")),
        name="memory_attention",
    )(proj3, mkv3, mkv3)


def _merge_body(y0_ref, y1_ref, y2_ref, glo_ref, ghi_ref, bg_ref, w_ref, o_ref, acc_ref):
    kb = pl.program_id(1)
    half = glo_ref.shape[1]

    def branch(y_ref):
        z = jnp.dot(y_ref[...], w_ref[...], preferred_element_type=F32)
        bg = bg_ref[...]
        g_lo = _sigmoid(glo_ref[...].astype(F32) + bg[:, :half])
        g_hi = _sigmoid(ghi_ref[...].astype(F32) + bg[:, half:])
        return jnp.concatenate([g_lo * z[:, :half], g_hi * z[:, half:]], axis=1)

    @pl.when(kb == 0)
    def _():
        acc_ref[...] = branch(y0_ref)

    @pl.when(kb == 1)
    def _():
        acc_ref[...] += branch(y1_ref)

    @pl.when(kb == 2)
    def _():
        o_ref[...] = (acc_ref[...] + branch(y2_ref)).astype(o_ref.dtype)


def _merge(y_lru, y_attn, y_mem, proj, b_gate, w_br, *, tm, gl_col0):
    t, d = y_lru.shape
    half = d // 2
    g0 = gl_col0 // half
    est = 6 * tm * d * 2 + 4 * tm * half * 2 + 2 * d * d * 2 + tm * d * 4 + 2 * tm * d * 2 + 4 * tm * d * 4
    ysp = pl.BlockSpec((tm, d), lambda i, k: (i, 0))
    return pl.pallas_call(
        _merge_body,
        grid=(t // tm, N_BRANCH),
        in_specs=[
            ysp, ysp, ysp,
            pl.BlockSpec((tm, half), lambda i, k: (i, g0 + 2 * k)),
            pl.BlockSpec((tm, half), lambda i, k: (i, g0 + 2 * k + 1)),
            pl.BlockSpec((1, d), lambda i, k: (0, k)),
            pl.BlockSpec((None, d, d), lambda i, k: (k, 0, 0)),
        ],
        out_specs=pl.BlockSpec((tm, d), lambda i, k: (i, 0)),
        out_shape=jax.ShapeDtypeStruct((t, d), BF16),
        scratch_shapes=[pltpu.VMEM((tm, d), F32)],
        compiler_params=pltpu.CompilerParams(
            dimension_semantics=("parallel", "arbitrary"), vmem_limit_bytes=_vmem_limit(est)),
        name="branch_merge",
    )(y_lru, y_attn, y_mem, proj, proj, b_gate.reshape(1, N_BRANCH * d), w_br)


def _out_proj_body(m_ref, w_ref, g_ref, x_ref, o_ref):
    z = jnp.dot(m_ref[...], w_ref[...], preferred_element_type=F32)
    o_ref[...] = x_ref[...] + _rms_scale(z) * g_ref[...]


def _out_proj(merged, w_out, g, x, *, tm):
    t, d = x.shape
    est = 2 * tm * d * 2 + 2 * d * d * 2 + 4 * tm * d * 4 + 3 * tm * d * 4
    return pl.pallas_call(
        _out_proj_body,
        grid=(t // tm,),
        in_specs=[
            pl.BlockSpec((tm, d), lambda i: (i, 0)),
            pl.BlockSpec((d, d), lambda i: (0, 0)),
            pl.BlockSpec((1, d), lambda i: (0, 0)),
            pl.BlockSpec((tm, d), lambda i: (i, 0)),
        ],
        out_specs=pl.BlockSpec((tm, d), lambda i: (i, 0)),
        out_shape=jax.ShapeDtypeStruct((t, d), F32),
        compiler_params=pltpu.CompilerParams(
            dimension_semantics=("parallel",), vmem_limit_bytes=_vmem_limit(est)),
        name="out_proj",
    )(merged, w_out, g.reshape(1, d), x)


def _mlp_body(x_ref, gpre_ref, wu_ref, wd_ref, gpost_ref, o_ref, h_ref, acc_ref):
    f = pl.program_id(1)

    @pl.when(f == 0)
    def _():
        h_ref[...] = (_rms_scale(x_ref[...]) * gpre_ref[...]).astype(h_ref.dtype)

    u = jnp.dot(h_ref[...], wu_ref[...], preferred_element_type=F32)
    u = jnp.square(jnp.maximum(u, 0.0)).astype(BF16)
    part = jnp.dot(u, wd_ref[...], preferred_element_type=F32)

    @pl.when(f == 0)
    def _():
        acc_ref[...] = part

    @pl.when(f > 0)
    def _():
        acc_ref[...] += part

    @pl.when(f == pl.num_programs(1) - 1)
    def _():
        o_ref[...] = x_ref[...] + _rms_scale(acc_ref[...]) * gpost_ref[...]


def _mlp(x, g_pre, w_up, w_down, g_post, *, tm, tf):
    t, d = x.shape
    ff = w_up.shape[1]
    est = 4 * tm * d * 4 + 4 * d * tf * 2 + tm * d * 2 + tm * d * 4 + 2 * tm * tf * 4 + 2 * tm * d * 4
    return pl.pallas_call(
        _mlp_body,
        grid=(t // tm, ff // tf),
        in_specs=[
            pl.BlockSpec((tm, d), lambda i, f: (i, 0)),
            pl.BlockSpec((1, d), lambda i, f: (0, 0)),
            pl.BlockSpec((d, tf), lambda i, f: (0, f)),
            pl.BlockSpec((tf, d), lambda i, f: (f, 0)),
            pl.BlockSpec((1, d), lambda i, f: (0, 0)),
        ],
        out_specs=pl.BlockSpec((tm, d), lambda i, f: (i, 0)),
        out_shape=jax.ShapeDtypeStruct((t, d), F32),
        scratch_shapes=[pltpu.VMEM((tm, d), BF16), pltpu.VMEM((tm, d), F32)],
        compiler_params=pltpu.CompilerParams(
            dimension_semantics=("parallel", "arbitrary"), vmem_limit_bytes=_vmem_limit(est)),
        name="mlp",
    )(x, g_pre.reshape(1, d), w_up, w_down, g_post.reshape(1, d))


def _layer(x, mem, positions, norm_mix_pre, norm_mix_post, norm_mem, w_in, b_gate,
           conv_w, conv_b, wr_f, br_f, wi_f, bi_f, lam_f, wr_b, br_b, wi_b, bi_b, lam_b,
           attn_sink, w_mem_kv, w_br_lru, w_br_attn, w_br_mem, w_out,
           norm_mlp_pre, norm_mlp_post, w_up, w_down):
    b, s, d = x.shape
    t = b * s
    nmem = mem.shape[1]
    bd = d // LRU_BLOCKS
    x2 = x.reshape(t, d)

    col_q = 2 * d
    col_k = col_q + N_Q_HEADS * HEAD_DIM
    col_v = col_k + N_KV_HEADS * HEAD_DIM
    col_qm = col_v + N_KV_HEADS * HEAD_DIM
    col_gl = col_qm + d
    n_in = col_gl + N_BRANCH * d
    assert w_in.shape == (d, n_in)

    proj = _norm_matmul(x2, norm_mix_pre, w_in.astype(BF16), tm=512, tn=1024, out_dtype=BF16, name="in_proj")
    proj3 = proj.reshape(b, s, n_in)

    cat = lambda a, c: jnp.concatenate([a, c], axis=-1)
    wf = cat(wr_f, wi_f).astype(BF16)
    wb = cat(wr_b, wi_b).astype(BF16)
    bf = cat(br_f, bi_f).reshape(LRU_BLOCKS, 1, 2 * bd)
    bb = cat(br_b, bi_b).reshape(LRU_BLOCKS, 1, 2 * bd)
    y_lru = _lru_branch(
        proj3, wf, wb, bf, bb, lam_f.reshape(LRU_BLOCKS, 1, bd), lam_b.reshape(LRU_BLOCKS, 1, bd),
        conv_w.reshape(CONV_WIDTH, LRU_BLOCKS, bd).transpose(1, 0, 2), conv_b.reshape(LRU_BLOCKS, 1, bd),
        ts=256)

    cos, sin = _rope_tables(positions, tr=1024)
    qk_cols = col_v - col_q
    qk = _rope_qk(proj, cos, sin, tr=1024, col0=col_q, ncols=qk_cols, wblk=Q_GROUP * HEAD_DIM)
    y_attn = _window_attention(
        qk.reshape(b, s, qk_cols), proj3, attn_sink, tq=256,
        k_blk0=((col_k - col_q) // HEAD_DIM,), v_blk0=col_v // HEAD_DIM)

    xd = d // N_X_HEADS
    mkv = _norm_matmul(mem.reshape(b * nmem, d), norm_mem, w_mem_kv.astype(BF16),
                       tm=b * nmem, tn=1024, out_dtype=BF16, name="mem_kv")
    y_mem = _memory_attention(proj3, mkv.reshape(b, nmem, 2 * d), tq=512, q_blk0=col_qm // xd)

    w_br = jnp.stack([w_br_lru, w_br_attn, w_br_mem]).astype(BF16)
    merged = _merge(y_lru.reshape(t, d), y_attn.reshape(t, d), y_mem.reshape(t, d), proj, b_gate, w_br,
                    tm=512, gl_col0=col_gl)
    x1 = _out_proj(merged, w_out.astype(BF16), norm_mix_post, x2, tm=512)

    out = _mlp(x1, norm_mlp_pre, w_up.astype(BF16), w_down.astype(BF16), norm_mlp_post, tm=512, tf=1024)
    return out.reshape(b, s, d)


def kernel(x, mem, positions, norm_mix_pre, norm_mix_post, norm_mem, w_in, b_gate, conv_w, conv_b, wr_f, br_f, wi_f, bi_f, lam_f, wr_b, br_b, wi_b, bi_b, lam_b, attn_sink, w_mem_kv, w_br_lru, w_br_attn, w_br_mem, w_out, norm_mlp_pre, norm_mlp_post, w_up, w_down):
    depth = w_in.shape[0]
    for l in range(depth):
        x = _layer(x, mem, positions, norm_mix_pre[l], norm_mix_post[l], norm_mem[l],
                   w_in[l], b_gate[l], conv_w[l], conv_b[l],
                   wr_f[l], br_f[l], wi_f[l], bi_f[l], lam_f[l],
                   wr_b[l], br_b[l], wi_b[l], bi_b[l], lam_b[l],
                   attn_sink[l], w_mem_kv[l], w_br_lru[l], w_br_attn[l], w_br_mem[l],
                   w_out[l], norm_mlp_pre[l], norm_mlp_post[l], w_up[l], w_down[l])
    return x
```

```python
import functools
import math

import jax
import jax.numpy as jnp
from jax import lax
from jax.experimental import pallas as pl
from jax.experimental.pallas import tpu as pltpu

LRU_BLOCKS = 16
LRU_C = 8.0
HEAD_DIM = 128
N_Q_HEADS = 16
N_KV_HEADS = 4
Q_GROUP = N_Q_HEADS // N_KV_HEADS
WINDOW = 128
N_X_HEADS = 4
N_BRANCH = 3
ROPE_THETA = 10000.0
EPS = 1e-6
CONV_WIDTH = 4

V7X_LANES = 128
V7X_SUBLANES = 8
V7X_BF16_SUBLANES = 16
V7X_VMEM_BYTES = 64 * 1024 * 1024

N_CHUNKS = V7X_SUBLANES

F32 = jnp.float32
BF16 = jnp.bfloat16
MASK_VALUE = -1e30


def _vmem_limit(nbytes):
    return int(min(nbytes, V7X_VMEM_BYTES - 4 * 1024 * 1024))


def _sigmoid(x):
    return 0.5 * jnp.tanh(0.5 * x) + 0.5


def _rms_scale(x):
    var = jnp.mean(x * x, axis=-1, keepdims=True)
    return x * lax.rsqrt(var + EPS)


def _gelu_tanh(x):
    return 0.5 * x * (1.0 + jnp.tanh(math.sqrt(2.0 / math.pi) * (x + 0.044715 * (x * x * x))))


def _norm_matmul_body(x_ref, g_ref, w_ref, cos_ref, sin_ref, o_ref, h_ref, *, rope_tiles):
    j = pl.program_id(1)

    @pl.when(j == 0)
    def _():
        h_ref[...] = (_rms_scale(x_ref[...]) * g_ref[...]).astype(h_ref.dtype)

    acc = jnp.dot(h_ref[...], w_ref[...], preferred_element_type=F32)
    nheads = acc.shape[1] // HEAD_DIM

    def write(n_rope):
        for h in range(nheads):
            sl = slice(h * HEAD_DIM, (h + 1) * HEAD_DIM)
            t = acc[:, sl]
            if h < n_rope:
                t = t * cos_ref[...] + pltpu.roll(t, HEAD_DIM // 2, 1) * sin_ref[...]
            o_ref[:, sl] = t.astype(o_ref.dtype)

    plain = None
    for tile, n_rope in rope_tiles:
        pl.when(j == tile)(functools.partial(write, n_rope))
        plain = (j != tile) if plain is None else plain & (j != tile)
    if plain is None:
        o_ref[...] = acc.astype(o_ref.dtype)
    else:
        @pl.when(plain)
        def _():
            o_ref[...] = acc.astype(o_ref.dtype)


def _norm_matmul(x, g, w, cos, sin, *, tm, tn, name, rope_tiles=()):
    m, d = x.shape
    n = w.shape[1]
    est = 2 * tm * d * 4 + 2 * d * tn * 2 + 2 * tm * tn * 2 + tm * d * 2 + 3 * tm * tn * 4 + tm * d * 4
    out_shape = jax.ShapeDtypeStruct((m, n), BF16)
    out_spec = pl.BlockSpec((tm, tn), lambda i, j: (i, j))
    return pl.pallas_call(
        functools.partial(_norm_matmul_body, rope_tiles=tuple(rope_tiles)),
        grid=(m // tm, n // tn),
        in_specs=[
            pl.BlockSpec((tm, d), lambda i, j: (i, 0)),
            pl.BlockSpec((1, d), lambda i, j: (0, 0)),
            pl.BlockSpec((d, tn), lambda i, j: (0, j)),
            pl.BlockSpec((tm, HEAD_DIM), lambda i, j: (i, 0)),
            pl.BlockSpec((tm, HEAD_DIM), lambda i, j: (i, 0)),
        ],
        out_specs=out_spec,
        out_shape=out_shape,
        scratch_shapes=[pltpu.VMEM((tm, d), BF16)],
        compiler_params=pltpu.CompilerParams(
            dimension_semantics=("parallel", "arbitrary"), vmem_limit_bytes=_vmem_limit(est)),
        name=name,
    )(x, g.reshape(1, d), w, cos, sin)


def _norm_matmul_chunked_body(x_ref, g_ref, w_ref, o_ref, h_ref):
    nch, tl, d = x_ref.shape

    @pl.when(pl.program_id(1) == 0)
    def _():
        x = x_ref[...].reshape(nch * tl, d)
        h_ref[...] = (_rms_scale(x) * g_ref[...]).astype(h_ref.dtype)

    acc = jnp.dot(h_ref[...], w_ref[...], preferred_element_type=F32)
    for s in range(nch):
        for c in range(o_ref.shape[0]):
            o_ref[c, pl.ds(s, tl, stride=nch), :] = acc[s * tl:(s + 1) * tl, c * V7X_LANES:(c + 1) * V7X_LANES]


def _norm_matmul_chunked(x3, g, w, *, tl, tn, name):
    b, seq, d = x3.shape
    n = w.shape[1]
    nch = N_CHUNKS
    clen = seq // nch
    tm = nch * tl
    est = 2 * tm * d * 4 + 2 * d * tn * 2 + 2 * tm * tn * 4 + tm * d * 2 + 2 * tm * tn * 4 + tm * d * 4
    return pl.pallas_call(
        _norm_matmul_chunked_body,
        grid=(b * (clen // tl), n // tn),
        in_specs=[
            pl.BlockSpec((None, nch, tl, d), lambda i, j: (i // (clen // tl), 0, i % (clen // tl), 0)),
            pl.BlockSpec((1, d), lambda i, j: (0, 0)),
            pl.BlockSpec((d, tn), lambda i, j: (0, j)),
        ],
        out_specs=pl.BlockSpec((None, tn // V7X_LANES, tm, V7X_LANES),
                               lambda i, j: (i // (clen // tl), j, i % (clen // tl), 0)),
        out_shape=jax.ShapeDtypeStruct((b, n // V7X_LANES, seq, V7X_LANES), F32),
        scratch_shapes=[pltpu.VMEM((tm, d), BF16)],
        compiler_params=pltpu.CompilerParams(
            dimension_semantics=("parallel", "arbitrary"), vmem_limit_bytes=_vmem_limit(est)),
        name=name,
    )(x3.reshape(b, nch, clen, d), g.reshape(1, d), w)


def _rope_table_body(pos_ref, freq_ref, sign_ref, cos_ref, sin_ref):
    ang = pos_ref[...].astype(F32) * freq_ref[...]
    cos_ref[...] = jnp.cos(ang)
    sin_ref[...] = jnp.sin(ang) * sign_ref[...]


def _rope_tables(positions, *, tr):
    t = positions.size
    half = HEAD_DIM // 2
    freqs = ROPE_THETA ** (-jnp.arange(half, dtype=F32) / half)
    freq2 = jnp.concatenate([freqs, freqs]).reshape(1, HEAD_DIM)
    sign = jnp.concatenate([-jnp.ones((half,), F32), jnp.ones((half,), F32)]).reshape(1, HEAD_DIM)
    tab = jax.ShapeDtypeStruct((t, HEAD_DIM), F32)
    return pl.pallas_call(
        _rope_table_body,
        grid=(t // tr,),
        in_specs=[
            pl.BlockSpec((tr, 1), lambda i: (i, 0)),
            pl.BlockSpec((1, HEAD_DIM), lambda i: (0, 0)),
            pl.BlockSpec((1, HEAD_DIM), lambda i: (0, 0)),
        ],
        out_specs=[pl.BlockSpec((tr, HEAD_DIM), lambda i: (i, 0))] * 2,
        out_shape=[tab, tab],
        compiler_params=pltpu.CompilerParams(dimension_semantics=("parallel",)),
        name="rope_tables",
    )(positions.reshape(t, 1), freq2, sign)


def _sublane_scan(av, uv, masks, reverse):
    for d, m in zip((1, 2, 4), masks):
        sh = V7X_SUBLANES - d if reverse else d
        us = jnp.where(m, 0.0, pltpu.roll(uv, sh, 0))
        as_ = jnp.where(m, 1.0, pltpu.roll(av, sh, 0))
        uv = uv + av * us
        av = av * as_
    return uv


def _lru_body(xr_ref, wf_ref, wb_ref, bf_ref, bb_ref, lamf_ref, lamb_ref, cw_ref, cb_ref,
              o_ref, p_ref, h2_ref, *, tl, unroll):
    rows = xr_ref.shape[0]
    sub = V7X_SUBLANES
    tr = tl * sub
    n = rows // tr
    h1_ref = o_ref
    cw = cw_ref[...]
    cb = cb_ref[...]
    row = lax.broadcasted_iota(jnp.int32, (sub, V7X_LANES), 0)
    masks_f = tuple(row < d for d in (1, 2, 4))
    masks_b = tuple(row >= sub - d for d in (1, 2, 4))

    def neg_c_softplus(lam_ref):
        z = -lam_ref[...]
        sp = jnp.maximum(z, 0.0) + jnp.log1p(jnp.exp(-jnp.abs(z)))
        return (0.5 * LRU_C) * sp

    def load(r0, nrows):
        return xr_ref[pl.ds(r0, nrows), :]

    def conv_tile(j):
        r0 = pl.multiple_of(j * tr, tr)
        cur = load(r0, tr)
        before = load(pl.multiple_of(jnp.maximum(r0 - sub, 0), sub), sub)
        after = load(pl.multiple_of(jnp.minimum(r0 + tr, rows - 2 * sub), sub), 2 * sub)
        last = load(rows - sub, sub)
        first = load(0, 2 * sub)
        wrap_m1 = jnp.where(masks_f[0], 0.0, pltpu.roll(last, 1, 0))
        wrap_p1 = jnp.where(masks_b[0], 0.0, pltpu.roll(first[:sub], sub - 1, 0))
        wrap_p2 = jnp.where(masks_b[0], 0.0, pltpu.roll(first[sub:], sub - 1, 0))
        xm1 = jnp.where(j > 0, before, wrap_m1)
        xp1 = jnp.where(j < n - 1, after[:sub], wrap_p1)
        xp2 = jnp.where(j < n - 1, after[sub:], wrap_p2)
        ext = jnp.concatenate([xm1, cur, xp1, xp2], axis=0)
        xc = (cw[0:1] * ext[0:tr] + cw[1:2] * cur
              + cw[2:3] * ext[2 * sub:2 * sub + tr] + cw[3:4] * ext[3 * sub:3 * sub + tr])
        return xc + cb, r0

    def gates(xc, w_ref, b_ref, c):
        g = jnp.tanh(jnp.dot(xc.astype(BF16), w_ref[...], preferred_element_type=F32) + b_ref[...])
        i = 0.5 * g[:, V7X_LANES:] + 0.5
        nla = c * g[:, :V7X_LANES] + c
        a = jnp.exp(-nla)
        one_minus_a2 = jnp.tanh(nla) * (a * a + 1.0)
        root = jnp.where(one_minus_a2 > 0.0, one_minus_a2 * lax.rsqrt(one_minus_a2), 0.0)
        u = root * (i * xc)
        return a, u

    def local_scan(a, u, h, p, reverse):
        hs, ps = [None] * tl, [None] * tl
        for q in (range(tl - 1, -1, -1) if reverse else range(tl)):
            av = a[q * sub:(q + 1) * sub]
            h = av * h + u[q * sub:(q + 1) * sub]
            p = p * av
            hs[q], ps[q] = h, p
        return jnp.concatenate(hs, axis=0), jnp.concatenate(ps, axis=0), h, p

    def chunk_carries(hend, ptot, masks, reverse):
        ends = _sublane_scan(ptot, hend, masks, reverse)
        return jnp.where(masks[0], 0.0, pltpu.roll(ends, sub - 1 if reverse else 1, 0))

    zeros = jnp.zeros((sub, V7X_LANES), F32)
    ones = jnp.ones((sub, V7X_LANES), F32)
    c_f = neg_c_softplus(lamf_ref)
    c_b = neg_c_softplus(lamb_ref)

    def fwd_step(j, carry):
        xc, r0 = conv_tile(j)
        a, u = gates(xc, wf_ref, bf_ref, c_f)
        hs, ps, h, p = local_scan(a, u, carry[0], carry[1], reverse=False)
        h1_ref[pl.ds(r0, tr), :] = hs
        p_ref[pl.ds(r0, tr), :] = ps
        return h, p

    hend, ptot = lax.fori_loop(0, n, fwd_step, (zeros, ones), unroll=unroll)
    carry_f = chunk_carries(hend, ptot, masks_f, reverse=False)

    def bwd_step(jj, carry):
        j = n - 1 - jj
        xc, r0 = conv_tile(j)
        a, u = gates(xc, wb_ref, bb_ref, c_b)
        sl = pl.ds(r0, tr)
        h1_ref[sl, :] = h1_ref[sl, :] + p_ref[sl, :] * jnp.concatenate([carry_f] * tl, axis=0)
        hs, ps, h, p = local_scan(a, u, carry[0], carry[1], reverse=True)
        h2_ref[sl, :] = hs
        p_ref[sl, :] = ps
        return h, p

    hend, ptot = lax.fori_loop(0, n, bwd_step, (zeros, ones), unroll=unroll)
    carry_b = chunk_carries(hend, ptot, masks_b, reverse=True)

    def out_step(j, _):
        sl = pl.ds(pl.multiple_of(j * tr, tr), tr)
        hb = h2_ref[sl, :] + p_ref[sl, :] * jnp.concatenate([carry_b] * tl, axis=0)
        o_ref[sl, :] = h1_ref[sl, :] + hb
        return 0

    lax.fori_loop(0, n, out_step, 0, unroll=unroll)


def _lru_branch(xr4, wf, wb, bf, bb, lam_f, lam_b, conv_w, conv_b, *, tl, unroll):
    b, c, rows, bd = xr4.shape
    est = 4 * rows * bd * 4 + 2 * rows * bd * 4 + 8 * 1024 * 1024
    blk = lambda shape: pl.BlockSpec((None,) + shape, lambda bi, ci: (ci,) + (0,) * len(shape))
    slab = pl.BlockSpec((None, None, rows, bd), lambda bi, ci: (bi, ci, 0, 0))
    return pl.pallas_call(
        functools.partial(_lru_body, tl=tl, unroll=unroll),
        grid=(b, c),
        in_specs=[
            slab,
            blk((bd, 2 * bd)), blk((bd, 2 * bd)),
            blk((1, 2 * bd)), blk((1, 2 * bd)),
            blk((1, bd)), blk((1, bd)),
            blk((CONV_WIDTH, bd)), blk((1, bd)),
        ],
        out_specs=slab,
        out_shape=jax.ShapeDtypeStruct(xr4.shape, F32),
        scratch_shapes=[pltpu.VMEM((rows, bd), F32)] * 2,
        compiler_params=pltpu.CompilerParams(
            dimension_semantics=("parallel", "parallel"), vmem_limit_bytes=_vmem_limit(est)),
        name="rglru",
    )(xr4, wf, wb, bf, bb, lam_f, lam_b, conv_w, conv_b)


def _win_attn_body(sink_ref, q_ref, kp_ref, kc_ref, kn_ref, vp_ref, vc_ref, vn_ref, o_ref, *, seq):
    g = pl.program_id(1)
    i = pl.program_id(2)
    tq = q_ref.shape[0]
    w = WINDOW
    nqb = tq // w
    nk = 3 * w
    k = jnp.concatenate([kp_ref[...], kc_ref[...], kn_ref[...]], axis=0)
    v = jnp.concatenate([vp_ref[...], vc_ref[...], vn_ref[...]], axis=0)
    keyi = lax.broadcasted_iota(jnp.int32, (nk, w), 0)
    qryi = lax.broadcasted_iota(jnp.int32, (nk, w), 1)
    band = jnp.where((keyi >= qryi) & (keyi <= qryi + 2 * w), 0.0, MASK_VALUE)
    sink = jnp.concatenate(
        [jnp.full((1, w), sink_ref[g * Q_GROUP + r], F32) for r in range(Q_GROUP)], axis=1)
    scale = HEAD_DIM ** -0.5
    for qb in range(nqb):
        rows = slice(qb * w, (qb + 1) * w)
        q = jnp.concatenate([q_ref[rows, r * HEAD_DIM:(r + 1) * HEAD_DIM] for r in range(Q_GROUP)], axis=0)
        kb = k[qb * w:qb * w + nk]
        vb = v[qb * w:qb * w + nk]
        mask = band
        if qb == 0 or qb == nqb - 1:
            kpos = keyi + (i * tq + (qb - 1) * w)
            mask = band + jnp.where((kpos >= 0) & (kpos < seq), 0.0, MASK_VALUE)
        st = lax.dot_general(kb, q, (((1,), (1,)), ((), ())), preferred_element_type=F32)
        st = st * scale + jnp.concatenate([mask] * Q_GROUP, axis=1)
        m = jnp.maximum(jnp.max(st, axis=0, keepdims=True), sink)
        p = jnp.exp(st - m)
        denom = jnp.sum(p, axis=0, keepdims=True) + jnp.exp(sink - m)
        ot = lax.dot_general(vb, p.astype(BF16), (((0,), (0,)), ((), ())), preferred_element_type=F32)
        ot = ot / denom
        for r in range(Q_GROUP):
            o_ref[rows, r * HEAD_DIM:(r + 1) * HEAD_DIM] = ot[:, r * w:(r + 1) * w].T.astype(o_ref.dtype)


def _window_attention(proj3, sink, *, tq, q_col0, k_col0, v_col0):
    b, s, _ = proj3.shape
    w = WINDOW
    per = tq // w
    nblk = s // w
    gw = Q_GROUP * HEAD_DIM
    qb, kb, vb = q_col0 // gw, k_col0 // HEAD_DIM, v_col0 // HEAD_DIM
    prev = lambda blk0: pl.BlockSpec(
        (None, w, HEAD_DIM), lambda bi, gi, i: (bi, jnp.maximum(i * per - 1, 0), blk0 + gi))
    cur = lambda blk0: pl.BlockSpec((None, tq, HEAD_DIM), lambda bi, gi, i: (bi, i, blk0 + gi))
    nxt = lambda blk0: pl.BlockSpec(
        (None, w, HEAD_DIM), lambda bi, gi, i: (bi, jnp.minimum((i + 1) * per, nblk - 1), blk0 + gi))
    return pl.pallas_call(
        functools.partial(_win_attn_body, seq=s),
        grid=(b, N_KV_HEADS, s // tq),
        in_specs=[
            pl.BlockSpec(memory_space=pltpu.SMEM),
            pl.BlockSpec((None, tq, gw), lambda bi, gi, i: (bi, i, qb + gi)),
            prev(kb), cur(kb), nxt(kb),
            prev(vb), cur(vb), nxt(vb),
        ],
        out_specs=pl.BlockSpec((None, tq, gw), lambda bi, gi, i: (bi, i, gi)),
        out_shape=jax.ShapeDtypeStruct((b, s, N_Q_HEADS * HEAD_DIM), BF16),
        compiler_params=pltpu.CompilerParams(dimension_semantics=("parallel", "parallel", "parallel")),
        name="window_attention",
    )(sink, proj3, proj3, proj3, proj3, proj3, proj3, proj3)


def _mem_attn_body(q_ref, mk_ref, mv_ref, o_ref):
    xd = q_ref.shape[1]
    s = lax.dot_general(q_ref[...], mk_ref[...], (((1,), (1,)), ((), ())), preferred_element_type=F32)
    s = s * (xd ** -0.5)
    m = jnp.max(s, axis=-1, keepdims=True)
    p = jnp.exp(s - m)
    denom = jnp.sum(p, axis=-1, keepdims=True)
    o = jnp.dot(p.astype(BF16), mv_ref[...], preferred_element_type=F32)
    o_ref[...] = (o / denom).astype(o_ref.dtype)


def _memory_attention(proj3, mkv3, *, tq, q_col0):
    b, s, _ = proj3.shape
    m = mkv3.shape[1]
    xd = mkv3.shape[2] // (2 * N_X_HEADS)
    q_blk0 = q_col0 // xd
    return pl.pallas_call(
        _mem_attn_body,
        grid=(b, s // tq, N_X_HEADS),
        in_specs=[
            pl.BlockSpec((None, tq, xd), lambda bi, i, h: (bi, i, q_blk0 + h)),
            pl.BlockSpec((None, m, xd), lambda bi, i, h: (bi, 0, h)),
            pl.BlockSpec((None, m, xd), lambda bi, i, h: (bi, 0, N_X_HEADS + h)),
        ],
        out_specs=pl.BlockSpec((None, tq, xd), lambda bi, i, h: (bi, i, h)),
        out_shape=jax.ShapeDtypeStruct((b, s, N_X_HEADS * xd), BF16),
        compiler_params=pltpu.CompilerParams(dimension_semantics=("parallel", "parallel", "parallel")),
        name="memory_attention",
    )(proj3, mkv3, mkv3)


def _merge_body(h_ref, gr_ref, y1_ref, y2_ref, g_ref, bg_ref, w_ref, o_ref, acc_ref, ylru_ref):
    kk = pl.program_id(1)
    nch, tl, d = y1_ref.shape
    tm = nch * tl
    half = g_ref.shape[2]

    @pl.when(kk == 0)
    def _():
        for s in range(nch):
            for c in range(h_ref.shape[0]):
                cs = slice(c * V7X_LANES, (c + 1) * V7X_LANES)
                hv = h_ref[c, pl.ds(s, tl, stride=nch), :]
                gate = _gelu_tanh(gr_ref[s, :, cs].astype(F32))
                ylru_ref[s * tl:(s + 1) * tl, cs] = (hv * gate).astype(ylru_ref.dtype)

    def gated(y):
        z = jnp.dot(y, w_ref[...], preferred_element_type=F32)
        return _sigmoid(g_ref[...].reshape(tm, half).astype(F32) + bg_ref[...]) * z

    for step in range(2 * N_BRANCH):
        br, cols = step // 2, slice((step % 2) * half, (step % 2 + 1) * half)

        @pl.when(kk == step)
        def _(br=br, cols=cols):
            if br == 0:
                acc_ref[:, cols] = gated(ylru_ref[...])
            elif br == 1:
                acc_ref[:, cols] += gated(y1_ref[...].reshape(tm, d))
            else:
                res = acc_ref[:, cols] + gated(y2_ref[...].reshape(tm, d))
                o_ref[:, :, cols] = res.astype(o_ref.dtype).reshape(nch, tl, half)


def _merge(h_lru4, y_attn3, y_mem3, proj3, b_gate, w_br, *, tl, gr_col0, gl_col0):
    b, seq, d = y_attn3.shape
    nch = N_CHUNKS
    clen = seq // nch
    tm = nch * tl
    half = d // 2
    g0 = gl_col0 // half
    nlb = clen // tl
    est = (2 * tm * d * 4 + 6 * tm * d * 2 + 2 * tm * half * 2 + 2 * d * half * 2 + tm * d * 4 + tm * d * 2
           + 2 * tm * d * 2 + 4 * tm * half * 4)
    chunked = lambda a: a.reshape(b, nch, clen, a.shape[-1])
    rows = lambda width, col: pl.BlockSpec((None, nch, tl, width), lambda i, k: (i // nlb, 0, i % nlb, col(k)))
    merged = pl.pallas_call(
        _merge_body,
        grid=(b * nlb, 2 * N_BRANCH),
        in_specs=[
            pl.BlockSpec((None, d // V7X_LANES, tm, V7X_LANES), lambda i, k: (i // nlb, 0, i % nlb, 0)),
            rows(d, lambda k: gr_col0 // d),
            rows(d, lambda k: 0), rows(d, lambda k: 0),
            rows(half, lambda k: g0 + k),
            pl.BlockSpec((1, half), lambda i, k: (0, k)),
            pl.BlockSpec((None, d, half), lambda i, k: (k // 2, 0, k % 2)),
        ],
        out_specs=rows(d, lambda k: 0),
        out_shape=jax.ShapeDtypeStruct((b, nch, clen, d), BF16),
        scratch_shapes=[pltpu.VMEM((tm, d), F32), pltpu.VMEM((tm, d), BF16)],
        compiler_params=pltpu.CompilerParams(
            dimension_semantics=("parallel", "arbitrary"), vmem_limit_bytes=_vmem_limit(est)),
        name="branch_merge",
    )(h_lru4, chunked(proj3), chunked(y_attn3), chunked(y_mem3), chunked(proj3),
      b_gate.reshape(1, N_BRANCH * d), w_br)
    return merged.reshape(b * seq, d)


def _out_proj_body(m_ref, w_ref, g_ref, x_ref, o_ref):
    z = jnp.dot(m_ref[...], w_ref[...], preferred_element_type=F32)
    o_ref[...] = x_ref[...] + _rms_scale(z) * g_ref[...]


def _out_proj(merged, w_out, g, x, *, tm):
    t, d = x.shape
    est = 2 * tm * d * 2 + 2 * d * d * 2 + 4 * tm * d * 4 + 3 * tm * d * 4
    return pl.pallas_call(
        _out_proj_body,
        grid=(t // tm,),
        in_specs=[
            pl.BlockSpec((tm, d), lambda i: (i, 0)),
            pl.BlockSpec((d, d), lambda i: (0, 0)),
            pl.BlockSpec((1, d), lambda i: (0, 0)),
            pl.BlockSpec((tm, d), lambda i: (i, 0)),
        ],
        out_specs=pl.BlockSpec((tm, d), lambda i: (i, 0)),
        out_shape=jax.ShapeDtypeStruct((t, d), F32),
        compiler_params=pltpu.CompilerParams(
            dimension_semantics=("parallel",), vmem_limit_bytes=_vmem_limit(est)),
        name="out_proj",
    )(merged, w_out, g.reshape(1, d), x)


def _mlp_body(x_ref, gpre_ref, wu_ref, wd_ref, gpost_ref, o_ref, h_ref):
    f = pl.program_id(1)

    @pl.when(f == 0)
    def _():
        h_ref[...] = (_rms_scale(x_ref[...]) * gpre_ref[...]).astype(h_ref.dtype)
        o_ref[...] = jnp.zeros_like(o_ref)

    u = jnp.dot(h_ref[...], wu_ref[...], preferred_element_type=F32)
    u = jnp.square(jnp.maximum(u, 0.0)).astype(BF16)
    o_ref[...] += jnp.dot(u, wd_ref[...], preferred_element_type=F32)

    @pl.when(f == pl.num_programs(1) - 1)
    def _():
        o_ref[...] = x_ref[...] + _rms_scale(o_ref[...]) * gpost_ref[...]


def _mlp(x, g_pre, w_up, w_down, g_post, *, tm, tf):
    t, d = x.shape
    ff = w_up.shape[1]
    est = 4 * tm * d * 4 + 4 * d * tf * 2 + tm * d * 2 + 2 * tm * tf * 4 + 2 * tm * d * 4
    return pl.pallas_call(
        _mlp_body,
        grid=(t // tm, ff // tf),
        in_specs=[
            pl.BlockSpec((tm, d), lambda i, f: (i, 0)),
            pl.BlockSpec((1, d), lambda i, f: (0, 0)),
            pl.BlockSpec((d, tf), lambda i, f: (0, f)),
            pl.BlockSpec((tf, d), lambda i, f: (f, 0)),
            pl.BlockSpec((1, d), lambda i, f: (0, 0)),
        ],
        out_specs=pl.BlockSpec((tm, d), lambda i, f: (i, 0)),
        out_shape=jax.ShapeDtypeStruct((t, d), F32),
        scratch_shapes=[pltpu.VMEM((tm, d), BF16)],
        compiler_params=pltpu.CompilerParams(
            dimension_semantics=("parallel", "arbitrary"), vmem_limit_bytes=_vmem_limit(est)),
        name="mlp",
    )(x, g_pre.reshape(1, d), w_up, w_down, g_post.reshape(1, d))


def _layer(x, mem, positions, norm_mix_pre, norm_mix_post, norm_mem, w_in, b_gate,
           conv_w, conv_b, wr_f, br_f, wi_f, bi_f, lam_f, wr_b, br_b, wi_b, bi_b, lam_b,
           attn_sink, w_mem_kv, w_br_lru, w_br_attn, w_br_mem, w_out,
           norm_mlp_pre, norm_mlp_post, w_up, w_down):
    b, s, d = x.shape
    t = b * s
    nmem = mem.shape[1]
    bd = d // LRU_BLOCKS
    x2 = x.reshape(t, d)
    tm, tn = 1024, 1024

    n_in = w_in.shape[1]
    col_gr = 0
    col_q = col_gr + d
    col_k = col_q + N_Q_HEADS * HEAD_DIM
    col_v = col_k + N_KV_HEADS * HEAD_DIM
    col_qm = col_v + N_KV_HEADS * HEAD_DIM
    col_gl = col_qm + d
    assert n_in == d + col_gl + N_BRANCH * d
    assert col_k % tn == 0 and (col_v - col_k) * 2 == tn

    cos, sin = _rope_tables(positions, tr=1024)
    w_in16 = w_in.astype(BF16)
    rope_tiles = [(jt, tn // HEAD_DIM) for jt in range(col_q // tn, col_k // tn)] + [(col_k // tn, N_KV_HEADS)]
    proj = _norm_matmul(x2, norm_mix_pre, w_in16[:, d:], cos, sin, tm=tm, tn=tn, name="in_proj",
                        rope_tiles=rope_tiles)
    proj3 = proj.reshape(b, s, n_in - d)
    xr4 = _norm_matmul_chunked(x, norm_mix_pre, w_in16[:, :d], tl=tm // N_CHUNKS, tn=tn, name="in_proj_lru")

    cat = lambda a, c: 0.5 * jnp.concatenate([a, c], axis=-1)
    wf = cat(wr_f, wi_f).astype(BF16)
    wb = cat(wr_b, wi_b).astype(BF16)
    bf = cat(br_f, bi_f).reshape(LRU_BLOCKS, 1, 2 * bd)
    bb = cat(br_b, bi_b).reshape(LRU_BLOCKS, 1, 2 * bd)
    h_lru = _lru_branch(
        xr4, wf, wb, bf, bb,
        lam_f.reshape(LRU_BLOCKS, 1, bd), lam_b.reshape(LRU_BLOCKS, 1, bd),
        conv_w.reshape(CONV_WIDTH, LRU_BLOCKS, bd).transpose(1, 0, 2), conv_b.reshape(LRU_BLOCKS, 1, bd),
        tl=32, unroll=4)

    y_attn = _window_attention(proj3, attn_sink, tq=512,q_col0=col_q, k_col0=col_k, v_col0=col_v)

    mkv = _norm_matmul(mem.reshape(b * nmem, d), norm_mem, w_mem_kv.astype(BF16), cos, sin,
                       tm=b * nmem, tn=tn, name="mem_kv")
    y_mem = _memory_attention(proj3, mkv.reshape(b, nmem, 2 * d), tq=512, q_col0=col_qm)

    w_br = jnp.stack([w_br_lru, w_br_attn, w_br_mem]).astype(BF16)
    merged = _merge(h_lru, y_attn, y_mem, proj3, b_gate, w_br, tl=64, gr_col0=col_gr, gl_col0=col_gl)
    x1 = _out_proj(merged, w_out.astype(BF16), norm_mix_post, x2, tm=512)

    out = _mlp(x1, norm_mlp_pre, w_up.astype(BF16), w_down.astype(BF16), norm_mlp_post, tm=1024, tf=512)
    return out.reshape(b, s, d)


def kernel(x, mem, positions, norm_mix_pre, norm_mix_post, norm_mem, w_in, b_gate, conv_w, conv_b, wr_f, br_f, wi_f, bi_f, lam_f, wr_b, br_b, wi_b, bi_b, lam_b, attn_sink, w_mem_kv, w_br_lru, w_br_attn, w_br_mem, w_out, norm_mlp_pre, norm_mlp_post, w_up, w_down):
    depth = w_in.shape[0]
    for l in range(depth):
        x = _layer(x, mem, positions, norm_mix_pre[l], norm_mix_post[l], norm_mem[l],
                   w_in[l], b_gate[l], conv_w[l], conv_b[l],
                   wr_f[l], br_f[l], wi_f[l], bi_f[l], lam_f[l],
                   wr_b[l], br_b[l], wi_b[l], bi_b[l], lam_b[l],
                   attn_sink[l], w_mem_kv[l], w_br_lru[l], w_br_attn[l], w_br_mem[l],
                   w_out[l], norm_mlp_pre[l], norm_mlp_post[l], w_up[l], w_down[l])
    return x
```

```python
import functools
import math

import jax
import jax.numpy as jnp
from jax import lax
from jax.experimental import pallas as pl
from jax.experimental.pallas import tpu as pltpu

LRU_BLOCKS = 16
LRU_C = 8.0
HEAD_DIM = 128
N_Q_HEADS = 16
N_KV_HEADS = 4
Q_GROUP = N_Q_HEADS // N_KV_HEADS
WINDOW = 128
N_X_HEADS = 4
N_BRANCH = 3
ROPE_THETA = 10000.0
EPS = 1e-6
CONV_WIDTH = 4

V7X_LANES = 128
V7X_SUBLANES = 8
V7X_BF16_SUBLANES = 16
V7X_VMEM_BYTES = 64 * 1024 * 1024

N_CHUNKS = V7X_SUBLANES

F32 = jnp.float32
BF16 = jnp.bfloat16
MASK_VALUE = -1e30


def _vmem_limit(nbytes):
    return int(min(nbytes, V7X_VMEM_BYTES - 4 * 1024 * 1024))


def _sigmoid(x):
    return 0.5 * jnp.tanh(0.5 * x) + 0.5


def _rms_scale(x):
    var = jnp.mean(x * x, axis=-1, keepdims=True)
    return x * lax.rsqrt(var + EPS)


def _gelu_tanh(x):
    return 0.5 * x * (1.0 + jnp.tanh(math.sqrt(2.0 / math.pi) * (x + 0.044715 * (x * x * x))))


def _norm_matmul_body(x_ref, g_ref, w_ref, cos_ref, sin_ref, o_ref, h_ref, *, rope_tiles):
    j = pl.program_id(1)

    @pl.when(j == 0)
    def _():
        h_ref[...] = (_rms_scale(x_ref[...]) * g_ref[...]).astype(h_ref.dtype)

    def compute(n_rope):
        acc = jnp.dot(h_ref[...], w_ref[...], preferred_element_type=F32)
        if n_rope == 0:
            o_ref[...] = acc.astype(o_ref.dtype)
            return
        for h in range(acc.shape[1] // HEAD_DIM):
            sl = slice(h * HEAD_DIM, (h + 1) * HEAD_DIM)
            t = acc[:, sl]
            if h < n_rope:
                t = t * cos_ref[...] + pltpu.roll(t, HEAD_DIM // 2, 1) * sin_ref[...]
            o_ref[:, sl] = t.astype(o_ref.dtype)

    plain = None
    for tile, n_rope in rope_tiles:
        pl.when(j == tile)(functools.partial(compute, n_rope))
        plain = (j != tile) if plain is None else plain & (j != tile)
    if plain is None:
        compute(0)
    else:
        pl.when(plain)(functools.partial(compute, 0))


def _norm_matmul(x, g, w, cos, sin, *, tm, tn, name, rope_tiles=()):
    m, d = x.shape
    n = w.shape[1]
    est = 2 * tm * d * 4 + 2 * d * tn * 2 + 2 * tm * tn * 2 + tm * d * 2 + 3 * tm * tn * 4 + tm * d * 4
    out_shape = jax.ShapeDtypeStruct((m, n), BF16)
    out_spec = pl.BlockSpec((tm, tn), lambda i, j: (i, j))
    return pl.pallas_call(
        functools.partial(_norm_matmul_body, rope_tiles=tuple(rope_tiles)),
        grid=(m // tm, n // tn),
        in_specs=[
            pl.BlockSpec((tm, d), lambda i, j: (i, 0)),
            pl.BlockSpec((1, d), lambda i, j: (0, 0)),
            pl.BlockSpec((d, tn), lambda i, j: (0, j)),
            pl.BlockSpec((tm, HEAD_DIM), lambda i, j: (i, 0)),
            pl.BlockSpec((tm, HEAD_DIM), lambda i, j: (i, 0)),
        ],
        out_specs=out_spec,
        out_shape=out_shape,
        scratch_shapes=[pltpu.VMEM((tm, d), BF16)],
        compiler_params=pltpu.CompilerParams(
            dimension_semantics=("parallel", "arbitrary"), vmem_limit_bytes=_vmem_limit(est)),
        name=name,
    )(x, g.reshape(1, d), w, cos, sin)


def _norm_matmul_chunked_body(x_ref, g_ref, w_ref, o_ref, h_ref):
    nch, tl, d = x_ref.shape

    @pl.when(pl.program_id(1) == 0)
    def _():
        x = x_ref[...].reshape(nch * tl, d)
        h_ref[...] = (_rms_scale(x) * g_ref[...]).astype(h_ref.dtype)

    acc = jnp.dot(h_ref[...], w_ref[...], preferred_element_type=F32)
    for s in range(nch):
        for c in range(o_ref.shape[0]):
            o_ref[c, pl.ds(s, tl, stride=nch), :] = acc[s * tl:(s + 1) * tl, c * V7X_LANES:(c + 1) * V7X_LANES]


def _norm_matmul_chunked(x3, g, w, *, tl, tn, name):
    b, seq, d = x3.shape
    n = w.shape[1]
    nch = N_CHUNKS
    clen = seq // nch
    tm = nch * tl
    est = 2 * tm * d * 4 + 2 * d * tn * 2 + 2 * tm * tn * 4 + tm * d * 2 + 2 * tm * tn * 4 + tm * d * 4
    return pl.pallas_call(
        _norm_matmul_chunked_body,
        grid=(b * (clen // tl), n // tn),
        in_specs=[
            pl.BlockSpec((None, nch, tl, d), lambda i, j: (i // (clen // tl), 0, i % (clen // tl), 0)),
            pl.BlockSpec((1, d), lambda i, j: (0, 0)),
            pl.BlockSpec((d, tn), lambda i, j: (0, j)),
        ],
        out_specs=pl.BlockSpec((None, tn // V7X_LANES, tm, V7X_LANES),
                               lambda i, j: (i // (clen // tl), j, i % (clen // tl), 0)),
        out_shape=jax.ShapeDtypeStruct((b, n // V7X_LANES, seq, V7X_LANES), F32),
        scratch_shapes=[pltpu.VMEM((tm, d), BF16)],
        compiler_params=pltpu.CompilerParams(
            dimension_semantics=("parallel", "arbitrary"), vmem_limit_bytes=_vmem_limit(est)),
        name=name,
    )(x3.reshape(b, nch, clen, d), g.reshape(1, d), w)


def _rope_table_body(pos_ref, freq_ref, sign_ref, cos_ref, sin_ref):
    ang = pos_ref[...].astype(F32) * freq_ref[...]
    cos_ref[...] = jnp.cos(ang)
    sin_ref[...] = jnp.sin(ang) * sign_ref[...]


def _rope_tables(positions, *, tr):
    t = positions.size
    half = HEAD_DIM // 2
    freqs = ROPE_THETA ** (-jnp.arange(half, dtype=F32) / half)
    freq2 = jnp.concatenate([freqs, freqs]).reshape(1, HEAD_DIM)
    sign = jnp.concatenate([-jnp.ones((half,), F32), jnp.ones((half,), F32)]).reshape(1, HEAD_DIM)
    tab = jax.ShapeDtypeStruct((t, HEAD_DIM), F32)
    return pl.pallas_call(
        _rope_table_body,
        grid=(t // tr,),
        in_specs=[
            pl.BlockSpec((tr, 1), lambda i: (i, 0)),
            pl.BlockSpec((1, HEAD_DIM), lambda i: (0, 0)),
            pl.BlockSpec((1, HEAD_DIM), lambda i: (0, 0)),
        ],
        out_specs=[pl.BlockSpec((tr, HEAD_DIM), lambda i: (i, 0))] * 2,
        out_shape=[tab, tab],
        compiler_params=pltpu.CompilerParams(dimension_semantics=("parallel",)),
        name="rope_tables",
    )(positions.reshape(t, 1), freq2, sign)


def _sublane_scan(av, uv, masks, reverse):
    for d, m in zip((1, 2, 4), masks):
        sh = V7X_SUBLANES - d if reverse else d
        us = jnp.where(m, 0.0, pltpu.roll(uv, sh, 0))
        as_ = jnp.where(m, 1.0, pltpu.roll(av, sh, 0))
        uv = uv + av * us
        av = av * as_
    return uv


def _lru_body(xr_ref, wf_ref, wb_ref, bf_ref, bb_ref, lamf_ref, lamb_ref, cw_ref, cb_ref,
              o_ref, p_ref, h2_ref, *, tl, unroll):
    rows = xr_ref.shape[0]
    sub = V7X_SUBLANES
    tr = tl * sub
    n = rows // tr
    h1_ref = o_ref
    cw = cw_ref[...]
    cb = cb_ref[...]
    row = lax.broadcasted_iota(jnp.int32, (sub, V7X_LANES), 0)
    masks_f = tuple(row < d for d in (1, 2, 4))
    masks_b = tuple(row >= sub - d for d in (1, 2, 4))

    def neg_c_softplus(lam_ref):
        z = -lam_ref[...]
        sp = jnp.maximum(z, 0.0) + jnp.log1p(jnp.exp(-jnp.abs(z)))
        return (0.5 * LRU_C) * sp

    def load(r0, nrows):
        return xr_ref[pl.ds(r0, nrows), :]

    def conv_tile(j):
        r0 = pl.multiple_of(j * tr, tr)
        cur = load(r0, tr)
        before = load(pl.multiple_of(jnp.maximum(r0 - sub, 0), sub), sub)
        after = load(pl.multiple_of(jnp.minimum(r0 + tr, rows - 2 * sub), sub), 2 * sub)
        last = load(rows - sub, sub)
        first = load(0, 2 * sub)
        wrap_m1 = jnp.where(masks_f[0], 0.0, pltpu.roll(last, 1, 0))
        wrap_p1 = jnp.where(masks_b[0], 0.0, pltpu.roll(first[:sub], sub - 1, 0))
        wrap_p2 = jnp.where(masks_b[0], 0.0, pltpu.roll(first[sub:], sub - 1, 0))
        xm1 = jnp.where(j > 0, before, wrap_m1)
        xp1 = jnp.where(j < n - 1, after[:sub], wrap_p1)
        xp2 = jnp.where(j < n - 1, after[sub:], wrap_p2)
        ext = jnp.concatenate([xm1, cur, xp1, xp2], axis=0)
        xc = (cw[0:1] * ext[0:tr] + cw[1:2] * cur
              + cw[2:3] * ext[2 * sub:2 * sub + tr] + cw[3:4] * ext[3 * sub:3 * sub + tr])
        return xc + cb, r0

    def gates(xc, w_ref, b_ref, c):
        g = jnp.tanh(jnp.dot(xc.astype(BF16), w_ref[...], preferred_element_type=F32) + b_ref[...])
        i = 0.5 * g[:, V7X_LANES:] + 0.5
        nla = c * g[:, :V7X_LANES] + c
        a = jnp.exp(-nla)
        one_minus_a2 = jnp.tanh(nla) * (a * a + 1.0)
        root = jnp.where(one_minus_a2 > 0.0, one_minus_a2 * lax.rsqrt(one_minus_a2), 0.0)
        u = root * (i * xc)
        return a, u

    def local_scan(a, u, h, p, reverse):
        hs, ps = [None] * tl, [None] * tl
        for q in (range(tl - 1, -1, -1) if reverse else range(tl)):
            av = a[q * sub:(q + 1) * sub]
            h = av * h + u[q * sub:(q + 1) * sub]
            p = p * av
            hs[q], ps[q] = h, p
        return jnp.concatenate(hs, axis=0), jnp.concatenate(ps, axis=0), h, p

    def chunk_carries(hend, ptot, masks, reverse):
        ends = _sublane_scan(ptot, hend, masks, reverse)
        return jnp.where(masks[0], 0.0, pltpu.roll(ends, sub - 1 if reverse else 1, 0))

    zeros = jnp.zeros((sub, V7X_LANES), F32)
    ones = jnp.ones((sub, V7X_LANES), F32)
    c_f = neg_c_softplus(lamf_ref)
    c_b = neg_c_softplus(lamb_ref)

    def fwd_step(j, carry):
        xc, r0 = conv_tile(j)
        a, u = gates(xc, wf_ref, bf_ref, c_f)
        hs, ps, h, p = local_scan(a, u, carry[0], carry[1], reverse=False)
        h1_ref[pl.ds(r0, tr), :] = hs
        p_ref[pl.ds(r0, tr), :] = ps
        return h, p

    hend, ptot = lax.fori_loop(0, n, fwd_step, (zeros, ones), unroll=unroll)
    carry_f = chunk_carries(hend, ptot, masks_f, reverse=False)

    def bwd_step(jj, carry):
        j = n - 1 - jj
        xc, r0 = conv_tile(j)
        a, u = gates(xc, wb_ref, bb_ref, c_b)
        sl = pl.ds(r0, tr)
        h1_ref[sl, :] = h1_ref[sl, :] + p_ref[sl, :] * jnp.concatenate([carry_f] * tl, axis=0)
        hs, ps, h, p = local_scan(a, u, carry[0], carry[1], reverse=True)
        h2_ref[sl, :] = hs
        p_ref[sl, :] = ps
        return h, p

    hend, ptot = lax.fori_loop(0, n, bwd_step, (zeros, ones), unroll=unroll)
    carry_b = chunk_carries(hend, ptot, masks_b, reverse=True)

    def out_step(j, _):
        sl = pl.ds(pl.multiple_of(j * tr, tr), tr)
        hb = h2_ref[sl, :] + p_ref[sl, :] * jnp.concatenate([carry_b] * tl, axis=0)
        o_ref[sl, :] = h1_ref[sl, :] + hb
        return 0

    lax.fori_loop(0, n, out_step, 0, unroll=unroll)


def _lru_branch(xr4, wf, wb, bf, bb, lam_f, lam_b, conv_w, conv_b, *, tl, unroll):
    b, c, rows, bd = xr4.shape
    est = 4 * rows * bd * 4 + 2 * rows * bd * 4 + 8 * 1024 * 1024
    blk = lambda shape: pl.BlockSpec((None,) + shape, lambda bi, ci: (ci,) + (0,) * len(shape))
    slab = pl.BlockSpec((None, None, rows, bd), lambda bi, ci: (bi, ci, 0, 0))
    return pl.pallas_call(
        functools.partial(_lru_body, tl=tl, unroll=unroll),
        grid=(b, c),
        in_specs=[
            slab,
            blk((bd, 2 * bd)), blk((bd, 2 * bd)),
            blk((1, 2 * bd)), blk((1, 2 * bd)),
            blk((1, bd)), blk((1, bd)),
            blk((CONV_WIDTH, bd)), blk((1, bd)),
        ],
        out_specs=slab,
        out_shape=jax.ShapeDtypeStruct(xr4.shape, F32),
        scratch_shapes=[pltpu.VMEM((rows, bd), F32)] * 2,
        compiler_params=pltpu.CompilerParams(
            dimension_semantics=("parallel", "parallel"), vmem_limit_bytes=_vmem_limit(est)),
        name="rglru",
    )(xr4, wf, wb, bf, bb, lam_f, lam_b, conv_w, conv_b)


def _win_attn_body(sink_ref, q_ref, kp_ref, kc_ref, kn_ref, vp_ref, vc_ref, vn_ref, o_ref, *, seq):
    g = pl.program_id(1)
    i = pl.program_id(2)
    tq = q_ref.shape[0]
    w = WINDOW
    nqb = tq // w
    nk = 3 * w
    k = jnp.concatenate([kp_ref[...], kc_ref[...], kn_ref[...]], axis=0)
    v = jnp.concatenate([vp_ref[...], vc_ref[...], vn_ref[...]], axis=0)
    keyi = lax.broadcasted_iota(jnp.int32, (nk, w), 0)
    qryi = lax.broadcasted_iota(jnp.int32, (nk, w), 1)
    band = jnp.where((keyi >= qryi) & (keyi <= qryi + 2 * w), 0.0, MASK_VALUE)
    sink = jnp.concatenate(
        [jnp.full((1, w), sink_ref[g * Q_GROUP + r], F32) for r in range(Q_GROUP)], axis=1)
    scale = HEAD_DIM ** -0.5
    for qb in range(nqb):
        rows = slice(qb * w, (qb + 1) * w)
        q = jnp.concatenate([q_ref[rows, r * HEAD_DIM:(r + 1) * HEAD_DIM] for r in range(Q_GROUP)], axis=0)
        kb = k[qb * w:qb * w + nk]
        vb = v[qb * w:qb * w + nk]
        mask = band
        if qb == 0 or qb == nqb - 1:
            kpos = keyi + (i * tq + (qb - 1) * w)
            mask = band + jnp.where((kpos >= 0) & (kpos < seq), 0.0, MASK_VALUE)
        st = lax.dot_general(kb, q, (((1,), (1,)), ((), ())), preferred_element_type=F32)
        st = st * scale + jnp.concatenate([mask] * Q_GROUP, axis=1)
        m = jnp.maximum(jnp.max(st, axis=0, keepdims=True), sink)
        p = jnp.exp(st - m)
        denom = jnp.sum(p, axis=0, keepdims=True) + jnp.exp(sink - m)
        ot = lax.dot_general(vb, p.astype(BF16), (((0,), (0,)), ((), ())), preferred_element_type=F32)
        ot = ot / denom
        for r in range(Q_GROUP):
            o_ref[rows, r * HEAD_DIM:(r + 1) * HEAD_DIM] = ot[:, r * w:(r + 1) * w].T.astype(o_ref.dtype)


def _window_attention(proj3, sink, *, tq, q_col0, k_col0, v_col0):
    b, s, _ = proj3.shape
    w = WINDOW
    per = tq // w
    nblk = s // w
    gw = Q_GROUP * HEAD_DIM
    qb, kb, vb = q_col0 // gw, k_col0 // HEAD_DIM, v_col0 // HEAD_DIM
    prev = lambda blk0: pl.BlockSpec(
        (None, w, HEAD_DIM), lambda bi, gi, i: (bi, jnp.maximum(i * per - 1, 0), blk0 + gi))
    cur = lambda blk0: pl.BlockSpec((None, tq, HEAD_DIM), lambda bi, gi, i: (bi, i, blk0 + gi))
    nxt = lambda blk0: pl.BlockSpec(
        (None, w, HEAD_DIM), lambda bi, gi, i: (bi, jnp.minimum((i + 1) * per, nblk - 1), blk0 + gi))
    return pl.pallas_call(
        functools.partial(_win_attn_body, seq=s),
        grid=(b, N_KV_HEADS, s // tq),
        in_specs=[
            pl.BlockSpec(memory_space=pltpu.SMEM),
            pl.BlockSpec((None, tq, gw), lambda bi, gi, i: (bi, i, qb + gi)),
            prev(kb), cur(kb), nxt(kb),
            prev(vb), cur(vb), nxt(vb),
        ],
        out_specs=pl.BlockSpec((None, tq, gw), lambda bi, gi, i: (bi, i, gi)),
        out_shape=jax.ShapeDtypeStruct((b, s, N_Q_HEADS * HEAD_DIM), BF16),
        compiler_params=pltpu.CompilerParams(dimension_semantics=("parallel", "parallel", "parallel")),
        name="window_attention",
    )(sink, proj3, proj3, proj3, proj3, proj3, proj3, proj3)


def _mem_attn_body(q_ref, mk_ref, mv_ref, o_ref):
    xd = q_ref.shape[1]
    s = lax.dot_general(q_ref[...], mk_ref[...], (((1,), (1,)), ((), ())), preferred_element_type=F32)
    s = s * (xd ** -0.5)
    m = jnp.max(s, axis=-1, keepdims=True)
    p = jnp.exp(s - m).astype(BF16)
    mv = jnp.concatenate([mv_ref[...], jnp.ones((mv_ref.shape[0], V7X_LANES), BF16)], axis=1)
    o = jnp.dot(p, mv, preferred_element_type=F32)
    inv = 1.0 / o[:, xd:]
    o_ref[...] = (o[:, :xd] * jnp.concatenate([inv] * (xd // V7X_LANES), axis=1)).astype(o_ref.dtype)


def _memory_attention(proj3, mkv3, *, tq, q_col0):
    b, s, _ = proj3.shape
    m = mkv3.shape[1]
    xd = mkv3.shape[2] // (2 * N_X_HEADS)
    q_blk0 = q_col0 // xd
    return pl.pallas_call(
        _mem_attn_body,
        grid=(b, s // tq, N_X_HEADS),
        in_specs=[
            pl.BlockSpec((None, tq, xd), lambda bi, i, h: (bi, i, q_blk0 + h)),
            pl.BlockSpec((None, m, xd), lambda bi, i, h: (bi, 0, h)),
            pl.BlockSpec((None, m, xd), lambda bi, i, h: (bi, 0, N_X_HEADS + h)),
        ],
        out_specs=pl.BlockSpec((None, tq, xd), lambda bi, i, h: (bi, i, h)),
        out_shape=jax.ShapeDtypeStruct((b, s, N_X_HEADS * xd), BF16),
        compiler_params=pltpu.CompilerParams(dimension_semantics=("parallel", "parallel", "parallel")),
        name="memory_attention",
    )(proj3, mkv3, mkv3)


def _merge_body(h_ref, gr_ref, y1_ref, y2_ref, *rest):
    g_refs, (bg_ref, w_ref, o_ref, ylru_ref) = rest[:2 * N_BRANCH], rest[2 * N_BRANCH:]
    nch, tl, d = y1_ref.shape
    tm = nch * tl
    half = d // 2

    for s in range(nch):
        for c in range(h_ref.shape[0]):
            cs = slice(c * V7X_LANES, (c + 1) * V7X_LANES)
            hv = h_ref[c, pl.ds(s, tl, stride=nch), :]
            gate = _gelu_tanh(gr_ref[s, :, cs].astype(F32))
            ylru_ref[s * tl:(s + 1) * tl, cs] = (hv * gate).astype(ylru_ref.dtype)

    ys = (ylru_ref[...], y1_ref[...].reshape(tm, d), y2_ref[...].reshape(tm, d))
    zs = [jnp.dot(ys[br], w_ref[br], preferred_element_type=F32) for br in range(N_BRANCH)]
    for hf in range(2):
        cols = slice(hf * half, (hf + 1) * half)
        acc = None
        for br in range(N_BRANCH):
            pre = g_refs[2 * br + hf][...].reshape(tm, half).astype(F32) + bg_ref[:, br * d + hf * half:br * d + (hf + 1) * half]
            term = _sigmoid(pre) * zs[br][:, cols]
            acc = term if acc is None else acc + term
        o_ref[:, :, cols] = acc.astype(o_ref.dtype).reshape(nch, tl, half)


def _merge(h_lru4, y_attn3, y_mem3, proj3, b_gate, w_br, *, tl, gr_col0, gl_col0):
    b, seq, d = y_attn3.shape
    nch = N_CHUNKS
    clen = seq // nch
    tm = nch * tl
    half = d // 2
    g0 = gl_col0 // half
    nlb = clen // tl
    est = (N_BRANCH * d * d * 2 + 2 * tm * d * 4 + 8 * tm * d * 2 + 4 * N_BRANCH * tm * half * 2 + tm * d * 2
           + 5 * tm * d * 4)
    chunked = lambda a: a.reshape(b, nch, clen, a.shape[-1])
    rows = lambda width, col: pl.BlockSpec((None, nch, tl, width), lambda i: (i // nlb, 0, i % nlb, col))
    proj4 = chunked(proj3)
    merged = pl.pallas_call(
        _merge_body,
        grid=(b * nlb,),
        in_specs=[
            pl.BlockSpec((None, d // V7X_LANES, tm, V7X_LANES), lambda i: (i // nlb, 0, i % nlb, 0)),
            rows(d, gr_col0 // d),
            rows(d, 0), rows(d, 0),
        ] + [rows(half, g0 + idx) for idx in range(2 * N_BRANCH)] + [
            pl.BlockSpec((1, N_BRANCH * d), lambda i: (0, 0)),
            pl.BlockSpec((N_BRANCH, d, d), lambda i: (0, 0, 0), pipeline_mode=pl.Buffered(1)),
        ],
        out_specs=rows(d, 0),
        out_shape=jax.ShapeDtypeStruct((b, nch, clen, d), BF16),
        scratch_shapes=[pltpu.VMEM((tm, d), BF16)],
        compiler_params=pltpu.CompilerParams(
            dimension_semantics=("parallel",), vmem_limit_bytes=_vmem_limit(est)),
        name="branch_merge",
    )(h_lru4, proj4, chunked(y_attn3), chunked(y_mem3), *([proj4] * (2 * N_BRANCH)),
      b_gate.reshape(1, N_BRANCH * d), w_br)
    return merged.reshape(b * seq, d)


def _out_proj_body(m_ref, w_ref, g_ref, x_ref, o_ref):
    z = jnp.dot(m_ref[...], w_ref[...], preferred_element_type=F32)
    o_ref[...] = x_ref[...] + _rms_scale(z) * g_ref[...]


def _out_proj(merged, w_out, g, x, *, tm):
    t, d = x.shape
    est = 2 * tm * d * 2 + 2 * d * d * 2 + 4 * tm * d * 4 + 3 * tm * d * 4
    return pl.pallas_call(
        _out_proj_body,
        grid=(t // tm,),
        in_specs=[
            pl.BlockSpec((tm, d), lambda i: (i, 0)),
            pl.BlockSpec((d, d), lambda i: (0, 0)),
            pl.BlockSpec((1, d), lambda i: (0, 0)),
            pl.BlockSpec((tm, d), lambda i: (i, 0)),
        ],
        out_specs=pl.BlockSpec((tm, d), lambda i: (i, 0)),
        out_shape=jax.ShapeDtypeStruct((t, d), F32),
        compiler_params=pltpu.CompilerParams(
            dimension_semantics=("parallel",), vmem_limit_bytes=_vmem_limit(est)),
        name="out_proj",
    )(merged, w_out, g.reshape(1, d), x)


def _mlp_body(x_ref, gpre_ref, wu_ref, wd_ref, gpost_ref, o_ref, h_ref):
    f = pl.program_id(1)

    @pl.when(f == 0)
    def _():
        h_ref[...] = (_rms_scale(x_ref[...]) * gpre_ref[...]).astype(h_ref.dtype)
        o_ref[...] = jnp.zeros_like(o_ref)

    u = jnp.dot(h_ref[...], wu_ref[...], preferred_element_type=F32)
    u = jnp.square(jnp.maximum(u, 0.0)).astype(BF16)
    o_ref[...] += jnp.dot(u, wd_ref[...], preferred_element_type=F32)

    @pl.when(f == pl.num_programs(1) - 1)
    def _():
        o_ref[...] = x_ref[...] + _rms_scale(o_ref[...]) * gpost_ref[...]


def _mlp(x, g_pre, w_up, w_down, g_post, *, tm, tf):
    t, d = x.shape
    ff = w_up.shape[1]
    est = 4 * tm * d * 4 + 4 * d * tf * 2 + tm * d * 2 + 2 * tm * tf * 4 + 2 * tm * d * 4
    return pl.pallas_call(
        _mlp_body,
        grid=(t // tm, ff // tf),
        in_specs=[
            pl.BlockSpec((tm, d), lambda i, f: (i, 0)),
            pl.BlockSpec((1, d), lambda i, f: (0, 0)),
            pl.BlockSpec((d, tf), lambda i, f: (0, f)),
            pl.BlockSpec((tf, d), lambda i, f: (f, 0)),
            pl.BlockSpec((1, d), lambda i, f: (0, 0)),
        ],
        out_specs=pl.BlockSpec((tm, d), lambda i, f: (i, 0)),
        out_shape=jax.ShapeDtypeStruct((t, d), F32),
        scratch_shapes=[pltpu.VMEM((tm, d), BF16)],
        compiler_params=pltpu.CompilerParams(
            dimension_semantics=("parallel", "arbitrary"), vmem_limit_bytes=_vmem_limit(est)),
        name="mlp",
    )(x, g_pre.reshape(1, d), w_up, w_down, g_post.reshape(1, d))


def _layer(x, mem, positions, norm_mix_pre, norm_mix_post, norm_mem, w_in, b_gate,
           conv_w, conv_b, wr_f, br_f, wi_f, bi_f, lam_f, wr_b, br_b, wi_b, bi_b, lam_b,
           attn_sink, w_mem_kv, w_br_lru, w_br_attn, w_br_mem, w_out,
           norm_mlp_pre, norm_mlp_post, w_up, w_down):
    b, s, d = x.shape
    t = b * s
    nmem = mem.shape[1]
    bd = d // LRU_BLOCKS
    x2 = x.reshape(t, d)
    tm, tn = 1024, 1024

    n_in = w_in.shape[1]
    col_gr = 0
    col_q = col_gr + d
    col_k = col_q + N_Q_HEADS * HEAD_DIM
    col_v = col_k + N_KV_HEADS * HEAD_DIM
    col_qm = col_v + N_KV_HEADS * HEAD_DIM
    col_gl = col_qm + d
    assert n_in == d + col_gl + N_BRANCH * d
    assert col_k % tn == 0 and (col_v - col_k) * 2 == tn

    cos, sin = _rope_tables(positions, tr=1024)
    w_in16 = w_in.astype(BF16)
    rope_tiles = [(jt, tn // HEAD_DIM) for jt in range(col_q // tn, col_k // tn)] + [(col_k // tn, N_KV_HEADS)]
    proj = _norm_matmul(x2, norm_mix_pre, w_in16[:, d:], cos, sin, tm=tm, tn=tn, name="in_proj",
                        rope_tiles=rope_tiles)
    proj3 = proj.reshape(b, s, n_in - d)
    xr4 = _norm_matmul_chunked(x, norm_mix_pre, w_in16[:, :d], tl=tm // N_CHUNKS, tn=tn, name="in_proj_lru")

    cat = lambda a, c: 0.5 * jnp.concatenate([a, c], axis=-1)
    wf = cat(wr_f, wi_f).astype(BF16)
    wb = cat(wr_b, wi_b).astype(BF16)
    bf = cat(br_f, bi_f).reshape(LRU_BLOCKS, 1, 2 * bd)
    bb = cat(br_b, bi_b).reshape(LRU_BLOCKS, 1, 2 * bd)
    h_lru = _lru_branch(
        xr4, wf, wb, bf, bb,
        lam_f.reshape(LRU_BLOCKS, 1, bd), lam_b.reshape(LRU_BLOCKS, 1, bd),
        conv_w.reshape(CONV_WIDTH, LRU_BLOCKS, bd).transpose(1, 0, 2), conv_b.reshape(LRU_BLOCKS, 1, bd),
        tl=32, unroll=4)

    y_attn = _window_attention(proj3, attn_sink, tq=512,q_col0=col_q, k_col0=col_k, v_col0=col_v)

    mkv = _norm_matmul(mem.reshape(b * nmem, d), norm_mem, w_mem_kv.astype(BF16), cos, sin,
                       tm=b * nmem, tn=tn, name="mem_kv")
    y_mem = _memory_attention(proj3, mkv.reshape(b, nmem, 2 * d), tq=1024, q_col0=col_qm)

    w_br = jnp.stack([w_br_lru, w_br_attn, w_br_mem]).astype(BF16)
    merged = _merge(h_lru, y_attn, y_mem, proj3, b_gate, w_br, tl=32, gr_col0=col_gr, gl_col0=col_gl)
    x1 = _out_proj(merged, w_out.astype(BF16), norm_mix_post, x2, tm=512)

    out = _mlp(x1, norm_mlp_pre, w_up.astype(BF16), w_down.astype(BF16), norm_mlp_post, tm=1024, tf=512)
    return out.reshape(b, s, d)


def kernel(x, mem, positions, norm_mix_pre, norm_mix_post, norm_mem, w_in, b_gate, conv_w, conv_b, wr_f, br_f, wi_f, bi_f, lam_f, wr_b, br_b, wi_b, bi_b, lam_b, attn_sink, w_mem_kv, w_br_lru, w_br_attn, w_br_mem, w_out, norm_mlp_pre, norm_mlp_post, w_up, w_down):
    depth = w_in.shape[0]
    for l in range(depth):
        x = _layer(x, mem, positions, norm_mix_pre[l], norm_mix_post[l], norm_mem[l],
                   w_in[l], b_gate[l], conv_w[l], conv_b[l],
                   wr_f[l], br_f[l], wi_f[l], bi_f[l], lam_f[l],
                   wr_b[l], br_b[l], wi_b[l], bi_b[l], lam_b[l],
                   attn_sink[l], w_mem_kv[l], w_br_lru[l], w_br_attn[l], w_br_mem[l],
                   w_out[l], norm_mlp_pre[l], norm_mlp_post[l], w_up[l], w_down[l])
    return x
```

```python
import functools
import math

import jax
import jax.numpy as jnp
from jax import lax
from jax.experimental import pallas as pl
from jax.experimental.pallas import tpu as pltpu

LRU_BLOCKS = 16
LRU_C = 8.0
HEAD_DIM = 128
N_Q_HEADS = 16
N_KV_HEADS = 4
Q_GROUP = N_Q_HEADS // N_KV_HEADS
WINDOW = 128
N_X_HEADS = 4
N_BRANCH = 3
ROPE_THETA = 10000.0
EPS = 1e-6
CONV_WIDTH = 4

V7X_LANES = 128
V7X_SUBLANES = 8
V7X_BF16_SUBLANES = 16
V7X_VMEM_BYTES = 64 * 1024 * 1024

N_CHUNKS = V7X_SUBLANES

F32 = jnp.float32
BF16 = jnp.bfloat16
MASK_VALUE = -1e30
LOG2E = math.log2(math.e)


def _vmem_limit(nbytes):
    return int(min(nbytes, V7X_VMEM_BYTES - 4 * 1024 * 1024))


def _sigmoid(x):
    return 0.5 * jnp.tanh(0.5 * x) + 0.5


def _rms_scale(x):
    var = jnp.mean(x * x, axis=-1, keepdims=True)
    return x * lax.rsqrt(var + EPS)


def _gelu_tanh(x):
    return 0.5 * x * (1.0 + jnp.tanh(math.sqrt(2.0 / math.pi) * (x + 0.044715 * (x * x * x))))


def _norm_matmul_body(x_ref, g_ref, w_ref, cos_ref, sin_ref, o_ref, h_ref, *, rope_tiles):
    j = pl.program_id(1)

    @pl.when(j == 0)
    def _():
        h_ref[...] = (_rms_scale(x_ref[...]) * g_ref[...]).astype(h_ref.dtype)

    def compute(n_rope):
        acc = jnp.dot(h_ref[...], w_ref[...], preferred_element_type=F32)
        if n_rope == 0:
            o_ref[...] = acc.astype(o_ref.dtype)
            return
        for h in range(acc.shape[1] // HEAD_DIM):
            sl = slice(h * HEAD_DIM, (h + 1) * HEAD_DIM)
            t = acc[:, sl]
            if h < n_rope:
                t = t * cos_ref[...] + pltpu.roll(t, HEAD_DIM // 2, 1) * sin_ref[...]
            o_ref[:, sl] = t.astype(o_ref.dtype)

    plain = None
    for tile, n_rope in rope_tiles:
        pl.when(j == tile)(functools.partial(compute, n_rope))
        plain = (j != tile) if plain is None else plain & (j != tile)
    if plain is None:
        compute(0)
    else:
        pl.when(plain)(functools.partial(compute, 0))


def _norm_matmul(x, g, w, cos, sin, *, tm, tn, name, rope_tiles=(), col0=0, ncols=None):
    m, d = x.shape
    n = w.shape[1] - col0 if ncols is None else ncols
    j0 = col0 // tn
    est = 2 * tm * d * 4 + 2 * d * tn * 2 + 2 * tm * tn * 2 + tm * d * 2 + 3 * tm * tn * 4 + tm * d * 4
    out_shape = jax.ShapeDtypeStruct((m, n), BF16)
    out_spec = pl.BlockSpec((tm, tn), lambda i, j: (i, j))
    return pl.pallas_call(
        functools.partial(_norm_matmul_body, rope_tiles=tuple(rope_tiles)),
        grid=(m // tm, n // tn),
        in_specs=[
            pl.BlockSpec((tm, d), lambda i, j: (i, 0)),
            pl.BlockSpec((1, d), lambda i, j: (0, 0)),
            pl.BlockSpec((d, tn), lambda i, j: (0, j0 + j)),
            pl.BlockSpec((tm, HEAD_DIM), lambda i, j: (i, 0)),
            pl.BlockSpec((tm, HEAD_DIM), lambda i, j: (i, 0)),
        ],
        out_specs=out_spec,
        out_shape=out_shape,
        scratch_shapes=[pltpu.VMEM((tm, d), BF16)],
        compiler_params=pltpu.CompilerParams(
            dimension_semantics=("parallel", "arbitrary"), vmem_limit_bytes=_vmem_limit(est)),
        name=name,
    )(x, g.reshape(1, d), w, cos, sin)


def _norm_matmul_chunked_body(x_ref, g_ref, w_ref, o_ref, h_ref):
    nch, tl, d = x_ref.shape

    @pl.when(pl.program_id(1) == 0)
    def _():
        x = x_ref[...].reshape(nch * tl, d)
        h_ref[...] = (_rms_scale(x) * g_ref[...]).astype(h_ref.dtype)

    acc = jnp.dot(h_ref[...], w_ref[...], preferred_element_type=F32)
    for s in range(nch):
        for c in range(o_ref.shape[0]):
            o_ref[c, pl.ds(s, tl, stride=nch), :] = acc[s * tl:(s + 1) * tl, c * V7X_LANES:(c + 1) * V7X_LANES]


def _norm_matmul_chunked(x3, g, w, *, tl, tn, ncols, name):
    b, seq, d = x3.shape
    n = ncols
    nch = N_CHUNKS
    clen = seq // nch
    tm = nch * tl
    est = 2 * tm * d * 4 + 2 * d * tn * 2 + 2 * tm * tn * 4 + tm * d * 2 + 2 * tm * tn * 4 + tm * d * 4
    return pl.pallas_call(
        _norm_matmul_chunked_body,
        grid=(b * (clen // tl), n // tn),
        in_specs=[
            pl.BlockSpec((None, nch, tl, d), lambda i, j: (i // (clen // tl), 0, i % (clen // tl), 0)),
            pl.BlockSpec((1, d), lambda i, j: (0, 0)),
            pl.BlockSpec((d, tn), lambda i, j: (0, j)),
        ],
        out_specs=pl.BlockSpec((None, tn // V7X_LANES, tm, V7X_LANES),
                               lambda i, j: (i // (clen // tl), j, i % (clen // tl), 0)),
        out_shape=jax.ShapeDtypeStruct((b, n // V7X_LANES, seq, V7X_LANES), F32),
        scratch_shapes=[pltpu.VMEM((tm, d), BF16)],
        compiler_params=pltpu.CompilerParams(
            dimension_semantics=("parallel", "arbitrary"), vmem_limit_bytes=_vmem_limit(est)),
        name=name,
    )(x3.reshape(b, nch, clen, d), g.reshape(1, d), w)


def _rope_table_body(pos_ref, freq_ref, sign_ref, cos_ref, sin_ref):
    ang = pos_ref[...].astype(F32) * freq_ref[...]
    cos_ref[...] = jnp.cos(ang)
    sin_ref[...] = jnp.sin(ang) * sign_ref[...]


def _rope_tables(positions, *, tr):
    t = positions.size
    half = HEAD_DIM // 2
    freqs = ROPE_THETA ** (-jnp.arange(half, dtype=F32) / half)
    freq2 = jnp.concatenate([freqs, freqs]).reshape(1, HEAD_DIM)
    sign = jnp.concatenate([-jnp.ones((half,), F32), jnp.ones((half,), F32)]).reshape(1, HEAD_DIM)
    tab = jax.ShapeDtypeStruct((t, HEAD_DIM), F32)
    return pl.pallas_call(
        _rope_table_body,
        grid=(t // tr,),
        in_specs=[
            pl.BlockSpec((tr, 1), lambda i: (i, 0)),
            pl.BlockSpec((1, HEAD_DIM), lambda i: (0, 0)),
            pl.BlockSpec((1, HEAD_DIM), lambda i: (0, 0)),
        ],
        out_specs=[pl.BlockSpec((tr, HEAD_DIM), lambda i: (i, 0))] * 2,
        out_shape=[tab, tab],
        compiler_params=pltpu.CompilerParams(dimension_semantics=("parallel",)),
        name="rope_tables",
    )(positions.reshape(t, 1), freq2, sign)


def _sublane_scan(av, uv, masks, reverse):
    for d, m in zip((1, 2, 4), masks):
        sh = V7X_SUBLANES - d if reverse else d
        us = jnp.where(m, 0.0, pltpu.roll(uv, sh, 0))
        as_ = jnp.where(m, 1.0, pltpu.roll(av, sh, 0))
        uv = uv + av * us
        av = av * as_
    return uv


def _lru_body(xr_ref, wf_ref, wb_ref, bf_ref, bb_ref, lamf_ref, lamb_ref, cw_ref, cb_ref,
              o_ref, p_ref, xc_ref, *, tl, unroll):
    rows = xr_ref.shape[0]
    sub = V7X_SUBLANES
    tr = tl * sub
    n = rows // tr
    cw = cw_ref[...]
    cb = cb_ref[...]
    row = lax.broadcasted_iota(jnp.int32, (sub, V7X_LANES), 0)
    masks_f = tuple(row < d for d in (1, 2, 4))
    masks_b = tuple(row >= sub - d for d in (1, 2, 4))

    def neg_c_softplus(lam_ref):
        z = -lam_ref[...]
        sp = jnp.maximum(z, 0.0) + jnp.log1p(jnp.exp(-jnp.abs(z)))
        return (0.5 * LRU_C) * sp

    def load(r0, nrows):
        return xr_ref[pl.ds(r0, nrows), :]

    def conv_tile(j):
        r0 = pl.multiple_of(j * tr, tr)
        cur = load(r0, tr)
        before = load(pl.multiple_of(jnp.maximum(r0 - sub, 0), sub), sub)
        after = load(pl.multiple_of(jnp.minimum(r0 + tr, rows - 2 * sub), sub), 2 * sub)
        last = load(rows - sub, sub)
        first = load(0, 2 * sub)
        wrap_m1 = jnp.where(masks_f[0], 0.0, pltpu.roll(last, 1, 0))
        wrap_p1 = jnp.where(masks_b[0], 0.0, pltpu.roll(first[:sub], sub - 1, 0))
        wrap_p2 = jnp.where(masks_b[0], 0.0, pltpu.roll(first[sub:], sub - 1, 0))
        xm1 = jnp.where(j > 0, before, wrap_m1)
        xp1 = jnp.where(j < n - 1, after[:sub], wrap_p1)
        xp2 = jnp.where(j < n - 1, after[sub:], wrap_p2)
        ext = jnp.concatenate([xm1, cur, xp1, xp2], axis=0)
        xc = (cw[0:1] * ext[0:tr] + cw[1:2] * cur
              + cw[2:3] * ext[2 * sub:2 * sub + tr] + cw[3:4] * ext[3 * sub:3 * sub + tr])
        return xc + cb, r0

    def gates(xc, w_ref, b_ref, c):
        g = jnp.tanh(jnp.dot(xc.astype(BF16), w_ref[...], preferred_element_type=F32) + b_ref[...])
        i = 0.5 * g[:, V7X_LANES:] + 0.5
        nla = c * g[:, :V7X_LANES] + c
        a = jnp.exp(-nla)
        one_minus_a2 = jnp.tanh(nla) * (a * a + 1.0)
        root = jnp.where(one_minus_a2 > 0.0, one_minus_a2 * lax.rsqrt(one_minus_a2), 0.0)
        u = root * (i * xc)
        return a, u

    def local_scan(a, u, h, p, reverse):
        hs, ps = [None] * tl, [None] * tl
        for q in (range(tl - 1, -1, -1) if reverse else range(tl)):
            av = a[q * sub:(q + 1) * sub]
            h = av * h + u[q * sub:(q + 1) * sub]
            p = p * av
            hs[q], ps[q] = h, p
        return jnp.concatenate(hs, axis=0), jnp.concatenate(ps, axis=0), h, p

    def chunk_carries(hend, ptot, masks, reverse):
        ends = _sublane_scan(ptot, hend, masks, reverse)
        return jnp.where(masks[0], 0.0, pltpu.roll(ends, sub - 1 if reverse else 1, 0))

    zeros = jnp.zeros((sub, V7X_LANES), F32)
    ones = jnp.ones((sub, V7X_LANES), F32)
    c_f = neg_c_softplus(lamf_ref)
    c_b = neg_c_softplus(lamb_ref)

    def fwd_step(j, carry):
        xc, r0 = conv_tile(j)
        a, u = gates(xc, wf_ref, bf_ref, c_f)
        hs, ps, h, p = local_scan(a, u, carry[0], carry[1], reverse=False)
        o_ref[pl.ds(r0, tr), :] = hs
        p_ref[pl.ds(r0, tr), :] = ps
        xc_ref[pl.ds(r0, tr), :] = xc
        return h, p

    hend, ptot = lax.fori_loop(0, n, fwd_step, (zeros, ones), unroll=unroll)
    carry_f = chunk_carries(hend, ptot, masks_f, reverse=False)

    def bwd_step(jj, carry):
        sl = pl.ds(pl.multiple_of((n - 1 - jj) * tr, tr), tr)
        a, u = gates(xc_ref[sl, :], wb_ref, bb_ref, c_b)
        hs, ps, h, p = local_scan(a, u, carry[0], carry[1], reverse=True)
        o_ref[sl, :] = o_ref[sl, :] + p_ref[sl, :] * jnp.concatenate([carry_f] * tl, axis=0) + hs
        p_ref[sl, :] = ps
        return h, p

    hend, ptot = lax.fori_loop(0, n, bwd_step, (zeros, ones), unroll=unroll)
    carry_b = chunk_carries(hend, ptot, masks_b, reverse=True)

    def out_step(j, _):
        sl = pl.ds(pl.multiple_of(j * tr, tr), tr)
        o_ref[sl, :] = o_ref[sl, :] + p_ref[sl, :] * jnp.concatenate([carry_b] * tl, axis=0)
        return 0

    lax.fori_loop(0, n, out_step, 0, unroll=unroll)


def _lru_branch(xr4, wf, wb, bf, bb, lam_f, lam_b, conv_w, conv_b, *, tl, unroll):
    b, c, rows, bd = xr4.shape
    est = 4 * rows * bd * 4 + 2 * rows * bd * 4 + 8 * 1024 * 1024
    blk = lambda shape: pl.BlockSpec((None,) + shape, lambda bi, ci: (ci,) + (0,) * len(shape))
    slab = pl.BlockSpec((None, None, rows, bd), lambda bi, ci: (bi, ci, 0, 0))
    return pl.pallas_call(
        functools.partial(_lru_body, tl=tl, unroll=unroll),
        grid=(b, c),
        in_specs=[
            slab,
            blk((bd, 2 * bd)), blk((bd, 2 * bd)),
            blk((1, 2 * bd)), blk((1, 2 * bd)),
            blk((1, bd)), blk((1, bd)),
            blk((CONV_WIDTH, bd)), blk((1, bd)),
        ],
        out_specs=slab,
        out_shape=jax.ShapeDtypeStruct(xr4.shape, F32),
        scratch_shapes=[pltpu.VMEM((rows, bd), F32)] * 2,
        compiler_params=pltpu.CompilerParams(
            dimension_semantics=("parallel", "parallel"), vmem_limit_bytes=_vmem_limit(est)),
        name="rglru",
    )(xr4, wf, wb, bf, bb, lam_f, lam_b, conv_w, conv_b)


def _win_attn_body(sink_ref, q_ref, kp_ref, kc_ref, kn_ref, vp_ref, vc_ref, vn_ref, o_ref, *, seq):
    i = pl.program_id(1)
    tq = q_ref.shape[0]
    w = WINDOW
    nqb = tq // w
    nk = 3 * w
    keyi = lax.broadcasted_iota(jnp.int32, (nk, w), 0)
    qryi = lax.broadcasted_iota(jnp.int32, (nk, w), 1)
    band = jnp.where((keyi >= qryi) & (keyi <= qryi + 2 * w), 0.0, MASK_VALUE)
    edge = {}
    for qb in {0, nqb - 1}:
        kpos = keyi + (i * tq + (qb - 1) * w)
        edge[qb] = band + jnp.where((kpos >= 0) & (kpos < seq), 0.0, MASK_VALUE)
    scale = HEAD_DIM ** -0.5 * LOG2E
    for g in range(N_KV_HEADS):
        gs = slice(g * HEAD_DIM, (g + 1) * HEAD_DIM)
        k = jnp.concatenate([kp_ref[:, gs], kc_ref[:, gs], kn_ref[:, gs]], axis=0)
        v = jnp.concatenate([vp_ref[:, gs], vc_ref[:, gs], vn_ref[:, gs]], axis=0)
        sink = jnp.concatenate(
            [jnp.full((1, w), sink_ref[g * Q_GROUP + r] * LOG2E, F32) for r in range(Q_GROUP)], axis=1)
        for qb in range(nqb):
            rows = slice(qb * w, (qb + 1) * w)
            heads = [slice((g * Q_GROUP + r) * HEAD_DIM, (g * Q_GROUP + r + 1) * HEAD_DIM) for r in range(Q_GROUP)]
            q = jnp.concatenate([q_ref[rows, hs] for hs in heads], axis=0)
            kb = k[qb * w:qb * w + nk]
            vb = v[qb * w:qb * w + nk]
            mask = edge.get(qb, band)
            st = lax.dot_general(kb, q, (((1,), (1,)), ((), ())), preferred_element_type=F32)
            st = st * scale + jnp.concatenate([mask] * Q_GROUP, axis=1)
            m = jnp.maximum(jnp.max(st, axis=0, keepdims=True), sink)
            p = jnp.exp2(st - m)
            denom = jnp.sum(p, axis=0, keepdims=True) + jnp.exp2(sink - m)
            ot = lax.dot_general(vb, p.astype(BF16), (((0,), (0,)), ((), ())), preferred_element_type=F32)
            ot = ot / denom
            for r, hs in enumerate(heads):
                o_ref[rows, hs] = ot[:, r * w:(r + 1) * w].T.astype(o_ref.dtype)


def _window_attention(proj3, sink, *, tq, q_col0, k_col0, v_col0):
    b, s, _ = proj3.shape
    w = WINDOW
    per = tq // w
    nblk = s // w
    qw, kw = N_Q_HEADS * HEAD_DIM, N_KV_HEADS * HEAD_DIM
    qb, kb, vb = q_col0 // qw, k_col0 // kw, v_col0 // kw
    prev = lambda blk: pl.BlockSpec((None, w, kw), lambda bi, i: (bi, jnp.maximum(i * per - 1, 0), blk))
    cur = lambda blk: pl.BlockSpec((None, tq, kw), lambda bi, i: (bi, i, blk))
    nxt = lambda blk: pl.BlockSpec((None, w, kw), lambda bi, i: (bi, jnp.minimum((i + 1) * per, nblk - 1), blk))
    return pl.pallas_call(
        functools.partial(_win_attn_body, seq=s),
        grid=(b, s // tq),
        in_specs=[
            pl.BlockSpec(memory_space=pltpu.SMEM),
            pl.BlockSpec((None, tq, qw), lambda bi, i: (bi, i, qb)),
            prev(kb), cur(kb), nxt(kb),
            prev(vb), cur(vb), nxt(vb),
        ],
        out_specs=pl.BlockSpec((None, tq, qw), lambda bi, i: (bi, i, 0)),
        out_shape=jax.ShapeDtypeStruct((b, s, qw), BF16),
        compiler_params=pltpu.CompilerParams(dimension_semantics=("parallel", "parallel")),
        name="window_attention",
    )(sink, proj3, proj3, proj3, proj3, proj3, proj3, proj3)


def _mem_attn_body(q_ref, mk_ref, mv_ref, o_ref):
    xd = q_ref.shape[1]
    s = lax.dot_general(q_ref[...], mk_ref[...], (((1,), (1,)), ((), ())), preferred_element_type=F32)
    s = s * (xd ** -0.5 * LOG2E)
    m = jnp.max(s, axis=-1, keepdims=True)
    p = jnp.exp2(s - m).astype(BF16)
    mv = jnp.concatenate([mv_ref[...], jnp.ones((mv_ref.shape[0], V7X_LANES), BF16)], axis=1)
    o = jnp.dot(p, mv, preferred_element_type=F32)
    inv = 1.0 / o[:, xd:]
    o_ref[...] = (o[:, :xd] * jnp.concatenate([inv] * (xd // V7X_LANES), axis=1)).astype(o_ref.dtype)


def _memory_attention(proj3, mkv3, *, tq, q_col0):
    b, s, _ = proj3.shape
    m = mkv3.shape[1]
    xd = mkv3.shape[2] // (2 * N_X_HEADS)
    q_blk0 = q_col0 // xd
    return pl.pallas_call(
        _mem_attn_body,
        grid=(b, s // tq, N_X_HEADS),
        in_specs=[
            pl.BlockSpec((None, tq, xd), lambda bi, i, h: (bi, i, q_blk0 + h)),
            pl.BlockSpec((None, m, xd), lambda bi, i, h: (bi, 0, h)),
            pl.BlockSpec((None, m, xd), lambda bi, i, h: (bi, 0, N_X_HEADS + h)),
        ],
        out_specs=pl.BlockSpec((None, tq, xd), lambda bi, i, h: (bi, i, h)),
        out_shape=jax.ShapeDtypeStruct((b, s, N_X_HEADS * xd), BF16),
        compiler_params=pltpu.CompilerParams(dimension_semantics=("parallel", "parallel", "parallel")),
        name="memory_attention",
    )(proj3, mkv3, mkv3)


def _merge_body(h_ref, gr_ref, y1_ref, y2_ref, *rest):
    g_refs, (bg_ref, w_ref, o_ref, ylru_ref) = rest[:2 * N_BRANCH], rest[2 * N_BRANCH:]
    nch, tl, d = y1_ref.shape
    tm = nch * tl
    half = d // 2

    for s in range(nch):
        for c in range(h_ref.shape[0]):
            cs = slice(c * V7X_LANES, (c + 1) * V7X_LANES)
            hv = h_ref[c, pl.ds(s, tl, stride=nch), :]
            gate = _gelu_tanh(gr_ref[s, :, cs].astype(F32))
            ylru_ref[s * tl:(s + 1) * tl, cs] = (hv * gate).astype(ylru_ref.dtype)

    ys = (ylru_ref[...], y1_ref[...].reshape(tm, d), y2_ref[...].reshape(tm, d))
    zs = [jnp.dot(ys[br], w_ref[br], preferred_element_type=F32) for br in range(N_BRANCH)]
    for hf in range(2):
        cols = slice(hf * half, (hf + 1) * half)
        acc = None
        for br in range(N_BRANCH):
            pre = g_refs[2 * br + hf][...].reshape(tm, half).astype(F32) + bg_ref[:, br * d + hf * half:br * d + (hf + 1) * half]
            term = _sigmoid(pre) * zs[br][:, cols]
            acc = term if acc is None else acc + term
        o_ref[:, :, cols] = acc.astype(o_ref.dtype).reshape(nch, tl, half)


def _merge(h_lru4, y_attn3, y_mem3, proj3, b_gate, w_br, *, tl, gr_col0, gl_col0):
    b, seq, d = y_attn3.shape
    nch = N_CHUNKS
    clen = seq // nch
    tm = nch * tl
    half = d // 2
    g0 = gl_col0 // half
    nlb = clen // tl
    est = (N_BRANCH * d * d * 2 + 2 * tm * d * 4 + 8 * tm * d * 2 + 4 * N_BRANCH * tm * half * 2 + tm * d * 2
           + 5 * tm * d * 4)
    chunked = lambda a: a.reshape(b, nch, clen, a.shape[-1])
    rows = lambda width, col: pl.BlockSpec((None, nch, tl, width), lambda i: (i // nlb, 0, i % nlb, col))
    proj4 = chunked(proj3)
    merged = pl.pallas_call(
        _merge_body,
        grid=(b * nlb,),
        in_specs=[
            pl.BlockSpec((None, d // V7X_LANES, tm, V7X_LANES), lambda i: (i // nlb, 0, i % nlb, 0)),
            rows(d, gr_col0 // d),
            rows(d, 0), rows(d, 0),
        ] + [rows(half, g0 + idx) for idx in range(2 * N_BRANCH)] + [
            pl.BlockSpec((1, N_BRANCH * d), lambda i: (0, 0)),
            pl.BlockSpec((N_BRANCH, d, d), lambda i: (0, 0, 0), pipeline_mode=pl.Buffered(1)),
        ],
        out_specs=rows(d, 0),
        out_shape=jax.ShapeDtypeStruct((b, nch, clen, d), BF16),
        scratch_shapes=[pltpu.VMEM((tm, d), BF16)],
        compiler_params=pltpu.CompilerParams(
            dimension_semantics=("parallel",), vmem_limit_bytes=_vmem_limit(est)),
        name="branch_merge",
    )(h_lru4, proj4, chunked(y_attn3), chunked(y_mem3), *([proj4] * (2 * N_BRANCH)),
      b_gate.reshape(1, N_BRANCH * d), w_br)
    return merged.reshape(b * seq, d)


def _out_proj_body(m_ref, w_ref, g_ref, x_ref, o_ref):
    z = jnp.dot(m_ref[...], w_ref[...], preferred_element_type=F32)
    o_ref[...] = x_ref[...] + _rms_scale(z) * g_ref[...]


def _out_proj(merged, w_out, g, x, *, tm):
    t, d = x.shape
    est = 2 * tm * d * 2 + 2 * d * d * 2 + 4 * tm * d * 4 + 3 * tm * d * 4
    return pl.pallas_call(
        _out_proj_body,
        grid=(t // tm,),
        in_specs=[
            pl.BlockSpec((tm, d), lambda i: (i, 0)),
            pl.BlockSpec((d, d), lambda i: (0, 0)),
            pl.BlockSpec((1, d), lambda i: (0, 0)),
            pl.BlockSpec((tm, d), lambda i: (i, 0)),
        ],
        out_specs=pl.BlockSpec((tm, d), lambda i: (i, 0)),
        out_shape=jax.ShapeDtypeStruct((t, d), F32),
        compiler_params=pltpu.CompilerParams(
            dimension_semantics=("parallel",), vmem_limit_bytes=_vmem_limit(est)),
        name="out_proj",
    )(merged, w_out, g.reshape(1, d), x)


def _mlp_body(x_ref, gpre_ref, wu_ref, wd_ref, gpost_ref, o_ref, h_ref):
    f = pl.program_id(1)

    @pl.when(f == 0)
    def _():
        h_ref[...] = (_rms_scale(x_ref[...]) * gpre_ref[...]).astype(h_ref.dtype)
        o_ref[...] = jnp.zeros_like(o_ref)

    u = jnp.dot(h_ref[...], wu_ref[...], preferred_element_type=F32)
    u = jnp.square(jnp.maximum(u, 0.0)).astype(BF16)
    o_ref[...] += jnp.dot(u, wd_ref[...], preferred_element_type=F32)

    @pl.when(f == pl.num_programs(1) - 1)
    def _():
        o_ref[...] = x_ref[...] + _rms_scale(o_ref[...]) * gpost_ref[...]


def _mlp(x, g_pre, w_up, w_down, g_post, *, tm, tf):
    t, d = x.shape
    ff = w_up.shape[1]
    est = 4 * tm * d * 4 + 4 * d * tf * 2 + tm * d * 2 + 2 * tm * tf * 4 + 2 * tm * d * 4
    return pl.pallas_call(
        _mlp_body,
        grid=(t // tm, ff // tf),
        in_specs=[
            pl.BlockSpec((tm, d), lambda i, f: (i, 0)),
            pl.BlockSpec((1, d), lambda i, f: (0, 0)),
            pl.BlockSpec((d, tf), lambda i, f: (0, f)),
            pl.BlockSpec((tf, d), lambda i, f: (f, 0)),
            pl.BlockSpec((1, d), lambda i, f: (0, 0)),
        ],
        out_specs=pl.BlockSpec((tm, d), lambda i, f: (i, 0)),
        out_shape=jax.ShapeDtypeStruct((t, d), F32),
        scratch_shapes=[pltpu.VMEM((tm, d), BF16)],
        compiler_params=pltpu.CompilerParams(
            dimension_semantics=("parallel", "arbitrary"), vmem_limit_bytes=_vmem_limit(est)),
        name="mlp",
    )(x, g_pre.reshape(1, d), w_up, w_down, g_post.reshape(1, d))


def _layer(x, mem, positions, norm_mix_pre, norm_mix_post, norm_mem, w_in, b_gate,
           conv_w, conv_b, wr_f, br_f, wi_f, bi_f, lam_f, wr_b, br_b, wi_b, bi_b, lam_b,
           attn_sink, w_mem_kv, w_br_lru, w_br_attn, w_br_mem, w_out,
           norm_mlp_pre, norm_mlp_post, w_up, w_down):
    b, s, d = x.shape
    t = b * s
    nmem = mem.shape[1]
    bd = d // LRU_BLOCKS
    x2 = x.reshape(t, d)
    tm, tn = 1024, 1024

    n_in = w_in.shape[1]
    col_gr = 0
    col_q = col_gr + d
    col_k = col_q + N_Q_HEADS * HEAD_DIM
    col_v = col_k + N_KV_HEADS * HEAD_DIM
    col_qm = col_v + N_KV_HEADS * HEAD_DIM
    col_gl = col_qm + d
    assert n_in == d + col_gl + N_BRANCH * d
    assert col_k % tn == 0 and (col_v - col_k) * 2 == tn

    cos, sin = _rope_tables(positions, tr=1024)
    w_in16 = w_in.astype(BF16)
    rope_tiles = [(jt, tn // HEAD_DIM) for jt in range(col_q // tn, col_k // tn)] + [(col_k // tn, N_KV_HEADS)]
    proj = _norm_matmul(x2, norm_mix_pre, w_in16, cos, sin, tm=tm, tn=tn, name="in_proj",
                        rope_tiles=rope_tiles, col0=d)
    proj3 = proj.reshape(b, s, n_in - d)
    xr4 = _norm_matmul_chunked(x, norm_mix_pre, w_in16, tl=tm // N_CHUNKS, tn=tn, ncols=d, name="in_proj_lru")

    cat = lambda a, c: 0.5 * jnp.concatenate([a, c], axis=-1)
    wf = cat(wr_f, wi_f).astype(BF16)
    wb = cat(wr_b, wi_b).astype(BF16)
    bf = cat(br_f, bi_f).reshape(LRU_BLOCKS, 1, 2 * bd)
    bb = cat(br_b, bi_b).reshape(LRU_BLOCKS, 1, 2 * bd)
    h_lru = _lru_branch(
        xr4, wf, wb, bf, bb,
        lam_f.reshape(LRU_BLOCKS, 1, bd), lam_b.reshape(LRU_BLOCKS, 1, bd),
        conv_w.reshape(CONV_WIDTH, LRU_BLOCKS, bd).transpose(1, 0, 2), conv_b.reshape(LRU_BLOCKS, 1, bd),
        tl=32, unroll=8)

    y_attn = _window_attention(proj3, attn_sink, tq=512,q_col0=col_q, k_col0=col_k, v_col0=col_v)

    mkv = _norm_matmul(mem.reshape(b * nmem, d), norm_mem, w_mem_kv.astype(BF16), cos, sin,
                       tm=b * nmem, tn=tn, name="mem_kv")
    y_mem = _memory_attention(proj3, mkv.reshape(b, nmem, 2 * d), tq=1024, q_col0=col_qm)

    w_br = jnp.stack([w_br_lru, w_br_attn, w_br_mem]).astype(BF16)
    merged = _merge(h_lru, y_attn, y_mem, proj3, b_gate, w_br, tl=32, gr_col0=col_gr, gl_col0=col_gl)
    x1 = _out_proj(merged, w_out.astype(BF16), norm_mix_post, x2, tm=512)

    out = _mlp(x1, norm_mlp_pre, w_up.astype(BF16), w_down.astype(BF16), norm_mlp_post, tm=1024, tf=512)
    return out.reshape(b, s, d)


def kernel(x, mem, positions, norm_mix_pre, norm_mix_post, norm_mem, w_in, b_gate, conv_w, conv_b, wr_f, br_f, wi_f, bi_f, lam_f, wr_b, br_b, wi_b, bi_b, lam_b, attn_sink, w_mem_kv, w_br_lru, w_br_attn, w_br_mem, w_out, norm_mlp_pre, norm_mlp_post, w_up, w_down):
    depth = w_in.shape[0]
    for l in range(depth):
        x = _layer(x, mem, positions, norm_mix_pre[l], norm_mix_post[l], norm_mem[l],
                   w_in[l], b_gate[l], conv_w[l], conv_b[l],
                   wr_f[l], br_f[l], wi_f[l], bi_f[l], lam_f[l],
                   wr_b[l], br_b[l], wi_b[l], bi_b[l], lam_b[l],
                   attn_sink[l], w_mem_kv[l], w_br_lru[l], w_br_attn[l], w_br_mem[l],
                   w_out[l], norm_mlp_pre[l], norm_mlp_post[l], w_up[l], w_down[l])
    return x
```

```python
import functools
import math

import jax
import jax.numpy as jnp
from jax import lax
from jax.experimental import pallas as pl
from jax.experimental.pallas import tpu as pltpu

LRU_BLOCKS = 16
LRU_C = 8.0
HEAD_DIM = 128
N_Q_HEADS = 16
N_KV_HEADS = 4
Q_GROUP = N_Q_HEADS // N_KV_HEADS
WINDOW = 128
N_X_HEADS = 4
N_BRANCH = 3
ROPE_THETA = 10000.0
EPS = 1e-6
CONV_WIDTH = 4

V7X_LANES = 128
V7X_SUBLANES = 8
V7X_BF16_SUBLANES = 16
V7X_VMEM_BYTES = 64 * 1024 * 1024

N_CHUNKS = V7X_SUBLANES

F32 = jnp.float32
BF16 = jnp.bfloat16
MASK_VALUE = -1e30
LOG2E = math.log2(math.e)


def _vmem_limit(nbytes):
    return int(min(nbytes, V7X_VMEM_BYTES - 4 * 1024 * 1024))


def _sigmoid(x):
    return 0.5 * jnp.tanh(0.5 * x) + 0.5


def _rms_scale(x):
    var = jnp.mean(x * x, axis=-1, keepdims=True)
    return x * lax.rsqrt(var + EPS)


def _gelu_tanh(x):
    return 0.5 * x * (1.0 + jnp.tanh(math.sqrt(2.0 / math.pi) * (x + 0.044715 * (x * x * x))))


def _norm_matmul_body(x_ref, g_ref, w_ref, cos_ref, sin_ref, o_ref, h_ref, *, rope_tiles):
    j = pl.program_id(1)

    @pl.when(j == 0)
    def _():
        h_ref[...] = (_rms_scale(x_ref[...]) * g_ref[...]).astype(h_ref.dtype)

    def compute(n_rope):
        acc = jnp.dot(h_ref[...], w_ref[...], preferred_element_type=F32)
        if n_rope == 0:
            o_ref[...] = acc.astype(o_ref.dtype)
            return
        for h in range(acc.shape[1] // HEAD_DIM):
            sl = slice(h * HEAD_DIM, (h + 1) * HEAD_DIM)
            t = acc[:, sl]
            if h < n_rope:
                t = t * cos_ref[...] + pltpu.roll(t, HEAD_DIM // 2, 1) * sin_ref[...]
            o_ref[:, sl] = t.astype(o_ref.dtype)

    plain = None
    for tile, n_rope in rope_tiles:
        pl.when(j == tile)(functools.partial(compute, n_rope))
        plain = (j != tile) if plain is None else plain & (j != tile)
    if plain is None:
        compute(0)
    else:
        pl.when(plain)(functools.partial(compute, 0))


def _norm_matmul(x, g, w, cos, sin, *, tm, tn, name, rope_tiles=(), col0=0, ncols=None):
    m, d = x.shape
    n = w.shape[1] - col0 if ncols is None else ncols
    j0 = col0 // tn
    est = 2 * tm * d * 4 + 2 * d * tn * 2 + 2 * tm * tn * 2 + tm * d * 2 + 3 * tm * tn * 4 + tm * d * 4
    out_shape = jax.ShapeDtypeStruct((m, n), BF16)
    out_spec = pl.BlockSpec((tm, tn), lambda i, j: (i, j))
    return pl.pallas_call(
        functools.partial(_norm_matmul_body, rope_tiles=tuple(rope_tiles)),
        grid=(m // tm, n // tn),
        in_specs=[
            pl.BlockSpec((tm, d), lambda i, j: (i, 0)),
            pl.BlockSpec((1, d), lambda i, j: (0, 0)),
            pl.BlockSpec((d, tn), lambda i, j: (0, j0 + j)),
            pl.BlockSpec((tm, HEAD_DIM), lambda i, j: (i, 0)),
            pl.BlockSpec((tm, HEAD_DIM), lambda i, j: (i, 0)),
        ],
        out_specs=out_spec,
        out_shape=out_shape,
        scratch_shapes=[pltpu.VMEM((tm, d), BF16)],
        compiler_params=pltpu.CompilerParams(
            dimension_semantics=("parallel", "arbitrary"), vmem_limit_bytes=_vmem_limit(est)),
        name=name,
    )(x, g.reshape(1, d), w, cos, sin)


def _norm_matmul_chunked_body(x_ref, g_ref, w_ref, o_ref, h_ref):
    nch, tl, d = x_ref.shape

    @pl.when(pl.program_id(1) == 0)
    def _():
        x = x_ref[...].reshape(nch * tl, d)
        h_ref[...] = (_rms_scale(x) * g_ref[...]).astype(h_ref.dtype)

    acc = jnp.dot(h_ref[...], w_ref[...], preferred_element_type=F32)
    for s in range(nch):
        for c in range(o_ref.shape[0]):
            o_ref[c, pl.ds(s, tl, stride=nch), :] = acc[s * tl:(s + 1) * tl, c * V7X_LANES:(c + 1) * V7X_LANES]


def _norm_matmul_chunked(x3, g, w, *, tl, tn, ncols, name):
    b, seq, d = x3.shape
    n = ncols
    nch = N_CHUNKS
    clen = seq // nch
    tm = nch * tl
    est = 2 * tm * d * 4 + 2 * d * tn * 2 + 2 * tm * tn * 4 + tm * d * 2 + 2 * tm * tn * 4 + tm * d * 4
    return pl.pallas_call(
        _norm_matmul_chunked_body,
        grid=(b * (clen // tl), n // tn),
        in_specs=[
            pl.BlockSpec((None, nch, tl, d), lambda i, j: (i // (clen // tl), 0, i % (clen // tl), 0)),
            pl.BlockSpec((1, d), lambda i, j: (0, 0)),
            pl.BlockSpec((d, tn), lambda i, j: (0, j)),
        ],
        out_specs=pl.BlockSpec((None, tn // V7X_LANES, tm, V7X_LANES),
                               lambda i, j: (i // (clen // tl), j, i % (clen // tl), 0)),
        out_shape=jax.ShapeDtypeStruct((b, n // V7X_LANES, seq, V7X_LANES), F32),
        scratch_shapes=[pltpu.VMEM((tm, d), BF16)],
        compiler_params=pltpu.CompilerParams(
            dimension_semantics=("parallel", "arbitrary"), vmem_limit_bytes=_vmem_limit(est)),
        name=name,
    )(x3.reshape(b, nch, clen, d), g.reshape(1, d), w)


def _rope_table_body(pos_ref, freq_ref, sign_ref, cos_ref, sin_ref):
    ang = pos_ref[...].astype(F32) * freq_ref[...]
    cos_ref[...] = jnp.cos(ang)
    sin_ref[...] = jnp.sin(ang) * sign_ref[...]


def _rope_tables(positions, *, tr):
    t = positions.size
    half = HEAD_DIM // 2
    freqs = ROPE_THETA ** (-jnp.arange(half, dtype=F32) / half)
    freq2 = jnp.concatenate([freqs, freqs]).reshape(1, HEAD_DIM)
    sign = jnp.concatenate([-jnp.ones((half,), F32), jnp.ones((half,), F32)]).reshape(1, HEAD_DIM)
    tab = jax.ShapeDtypeStruct((t, HEAD_DIM), F32)
    return pl.pallas_call(
        _rope_table_body,
        grid=(t // tr,),
        in_specs=[
            pl.BlockSpec((tr, 1), lambda i: (i, 0)),
            pl.BlockSpec((1, HEAD_DIM), lambda i: (0, 0)),
            pl.BlockSpec((1, HEAD_DIM), lambda i: (0, 0)),
        ],
        out_specs=[pl.BlockSpec((tr, HEAD_DIM), lambda i: (i, 0))] * 2,
        out_shape=[tab, tab],
        compiler_params=pltpu.CompilerParams(dimension_semantics=("parallel",)),
        name="rope_tables",
    )(positions.reshape(t, 1), freq2, sign)


def _sublane_scan(av, uv, masks, reverse):
    for d, m in zip((1, 2, 4), masks):
        sh = V7X_SUBLANES - d if reverse else d
        us = jnp.where(m, 0.0, pltpu.roll(uv, sh, 0))
        as_ = jnp.where(m, 1.0, pltpu.roll(av, sh, 0))
        uv = uv + av * us
        av = av * as_
    return uv


def _lru_body(xr_ref, wf_ref, wb_ref, bf_ref, bb_ref, lamf_ref, lamb_ref, cw_ref, cb_ref,
              o_ref, p_ref, xc_ref, *, tl, unroll):
    rows = xr_ref.shape[0]
    sub = V7X_SUBLANES
    tr = tl * sub
    n = rows // tr
    cw = cw_ref[...]
    cb = cb_ref[...]
    row = lax.broadcasted_iota(jnp.int32, (sub, V7X_LANES), 0)
    masks_f = tuple(row < d for d in (1, 2, 4))
    masks_b = tuple(row >= sub - d for d in (1, 2, 4))

    def neg_c_softplus(lam_ref):
        z = -lam_ref[...]
        sp = jnp.maximum(z, 0.0) + jnp.log1p(jnp.exp(-jnp.abs(z)))
        return (0.5 * LRU_C) * sp

    def load(r0, nrows):
        return xr_ref[pl.ds(r0, nrows), :]

    def conv_tile(j):
        r0 = pl.multiple_of(j * tr, tr)
        cur = load(r0, tr)
        before = load(pl.multiple_of(jnp.maximum(r0 - sub, 0), sub), sub)
        after = load(pl.multiple_of(jnp.minimum(r0 + tr, rows - 2 * sub), sub), 2 * sub)
        last = load(rows - sub, sub)
        first = load(0, 2 * sub)
        wrap_m1 = jnp.where(masks_f[0], 0.0, pltpu.roll(last, 1, 0))
        wrap_p1 = jnp.where(masks_b[0], 0.0, pltpu.roll(first[:sub], sub - 1, 0))
        wrap_p2 = jnp.where(masks_b[0], 0.0, pltpu.roll(first[sub:], sub - 1, 0))
        xm1 = jnp.where(j > 0, before, wrap_m1)
        xp1 = jnp.where(j < n - 1, after[:sub], wrap_p1)
        xp2 = jnp.where(j < n - 1, after[sub:], wrap_p2)
        ext = jnp.concatenate([xm1, cur, xp1, xp2], axis=0)
        xc = (cw[0:1] * ext[0:tr] + cw[1:2] * cur
              + cw[2:3] * ext[2 * sub:2 * sub + tr] + cw[3:4] * ext[3 * sub:3 * sub + tr])
        return xc + cb, r0

    def gates(xc, w_ref, b_ref, c):
        g = jnp.tanh(jnp.dot(xc.astype(BF16), w_ref[...], preferred_element_type=F32) + b_ref[...])
        i = 0.5 * g[:, V7X_LANES:] + 0.5
        nla = c * g[:, :V7X_LANES] + c
        a = jnp.exp(-nla)
        one_minus_a2 = jnp.tanh(nla) * (a * a + 1.0)
        root = jnp.where(one_minus_a2 > 0.0, one_minus_a2 * lax.rsqrt(one_minus_a2), 0.0)
        u = root * (i * xc)
        return a, u

    def local_scan(a, u, h, p, reverse):
        hs, ps = [None] * tl, [None] * tl
        for q in (range(tl - 1, -1, -1) if reverse else range(tl)):
            av = a[q * sub:(q + 1) * sub]
            h = av * h + u[q * sub:(q + 1) * sub]
            p = p * av
            hs[q], ps[q] = h, p
        return jnp.concatenate(hs, axis=0), jnp.concatenate(ps, axis=0), h, p

    def chunk_carries(hend, ptot, masks, reverse):
        ends = _sublane_scan(ptot, hend, masks, reverse)
        return jnp.where(masks[0], 0.0, pltpu.roll(ends, sub - 1 if reverse else 1, 0))

    zeros = jnp.zeros((sub, V7X_LANES), F32)
    ones = jnp.ones((sub, V7X_LANES), F32)
    c_f = neg_c_softplus(lamf_ref)
    c_b = neg_c_softplus(lamb_ref)

    def fwd_step(j, carry):
        xc, r0 = conv_tile(j)
        a, u = gates(xc, wf_ref, bf_ref, c_f)
        hs, ps, h, p = local_scan(a, u, carry[0], carry[1], reverse=False)
        o_ref[pl.ds(r0, tr), :] = hs
        p_ref[pl.ds(r0, tr), :] = ps
        xc_ref[pl.ds(r0, tr), :] = xc
        return h, p

    hend, ptot = lax.fori_loop(0, n, fwd_step, (zeros, ones), unroll=unroll)
    carry_f = chunk_carries(hend, ptot, masks_f, reverse=False)

    def bwd_step(jj, carry):
        sl = pl.ds(pl.multiple_of((n - 1 - jj) * tr, tr), tr)
        a, u = gates(xc_ref[sl, :], wb_ref, bb_ref, c_b)
        hs, ps, h, p = local_scan(a, u, carry[0], carry[1], reverse=True)
        o_ref[sl, :] = o_ref[sl, :] + p_ref[sl, :] * jnp.concatenate([carry_f] * tl, axis=0) + hs
        p_ref[sl, :] = ps
        return h, p

    hend, ptot = lax.fori_loop(0, n, bwd_step, (zeros, ones), unroll=unroll)
    carry_b = chunk_carries(hend, ptot, masks_b, reverse=True)

    def out_step(j, _):
        sl = pl.ds(pl.multiple_of(j * tr, tr), tr)
        o_ref[sl, :] = o_ref[sl, :] + p_ref[sl, :] * jnp.concatenate([carry_b] * tl, axis=0)
        return 0

    lax.fori_loop(0, n, out_step, 0, unroll=unroll)


def _lru_branch(xr4, wf, wb, bf, bb, lam_f, lam_b, conv_w, conv_b, *, tl, unroll):
    b, c, rows, bd = xr4.shape
    est = 4 * rows * bd * 4 + 2 * rows * bd * 4 + 8 * 1024 * 1024
    blk = lambda shape: pl.BlockSpec((None,) + shape, lambda bi, ci: (ci,) + (0,) * len(shape))
    slab = pl.BlockSpec((None, None, rows, bd), lambda bi, ci: (bi, ci, 0, 0))
    return pl.pallas_call(
        functools.partial(_lru_body, tl=tl, unroll=unroll),
        grid=(b, c),
        in_specs=[
            slab,
            blk((bd, 2 * bd)), blk((bd, 2 * bd)),
            blk((1, 2 * bd)), blk((1, 2 * bd)),
            blk((1, bd)), blk((1, bd)),
            blk((CONV_WIDTH, bd)), blk((1, bd)),
        ],
        out_specs=slab,
        out_shape=jax.ShapeDtypeStruct(xr4.shape, F32),
        scratch_shapes=[pltpu.VMEM((rows, bd), F32)] * 2,
        compiler_params=pltpu.CompilerParams(
            dimension_semantics=("parallel", "parallel"), vmem_limit_bytes=_vmem_limit(est)),
        name="rglru",
    )(xr4, wf, wb, bf, bb, lam_f, lam_b, conv_w, conv_b)


def _win_attn_body(sink_ref, q_ref, kp_ref, kc_ref, kn_ref, vp_ref, vc_ref, vn_ref, o_ref, *, seq):
    i = pl.program_id(1)
    tq = q_ref.shape[0]
    w = WINDOW
    nqb = tq // w
    nk = 3 * w
    keyi = lax.broadcasted_iota(jnp.int32, (nk, w), 0)
    qryi = lax.broadcasted_iota(jnp.int32, (nk, w), 1)
    band = jnp.where((keyi >= qryi) & (keyi <= qryi + 2 * w), 0.0, MASK_VALUE)
    edge = {}
    for qb in {0, nqb - 1}:
        kpos = keyi + (i * tq + (qb - 1) * w)
        edge[qb] = band + jnp.where((kpos >= 0) & (kpos < seq), 0.0, MASK_VALUE)
    scale = HEAD_DIM ** -0.5 * LOG2E
    for g in range(N_KV_HEADS):
        gs = slice(g * HEAD_DIM, (g + 1) * HEAD_DIM)
        k = jnp.concatenate([kp_ref[:, gs], kc_ref[:, gs], kn_ref[:, gs]], axis=0)
        v = jnp.concatenate([vp_ref[:, gs], vc_ref[:, gs], vn_ref[:, gs]], axis=0)
        sink = jnp.concatenate(
            [jnp.full((1, w), sink_ref[g * Q_GROUP + r] * LOG2E, F32) for r in range(Q_GROUP)], axis=1)
        for qb in range(nqb):
            rows = slice(qb * w, (qb + 1) * w)
            heads = [slice((g * Q_GROUP + r) * HEAD_DIM, (g * Q_GROUP + r + 1) * HEAD_DIM) for r in range(Q_GROUP)]
            q = jnp.concatenate([q_ref[rows, hs] for hs in heads], axis=0)
            kb = k[qb * w:qb * w + nk]
            vb = v[qb * w:qb * w + nk]
            mask = edge.get(qb, band)
            st = lax.dot_general(kb, q, (((1,), (1,)), ((), ())), preferred_element_type=F32)
            st = st * scale + jnp.concatenate([mask] * Q_GROUP, axis=1)
            m = jnp.maximum(jnp.max(st, axis=0, keepdims=True), sink)
            p = jnp.exp2(st - m)
            denom = jnp.sum(p, axis=0, keepdims=True) + jnp.exp2(sink - m)
            ot = lax.dot_general(vb, p.astype(BF16), (((0,), (0,)), ((), ())), preferred_element_type=F32)
            ot = ot / denom
            for r, hs in enumerate(heads):
                o_ref[rows, hs] = ot[:, r * w:(r + 1) * w].T.astype(o_ref.dtype)


def _window_attention(proj3, sink, *, tq, q_col0, k_col0, v_col0):
    b, s, _ = proj3.shape
    w = WINDOW
    per = tq // w
    nblk = s // w
    qw, kw = N_Q_HEADS * HEAD_DIM, N_KV_HEADS * HEAD_DIM
    qb, kb, vb = q_col0 // qw, k_col0 // kw, v_col0 // kw
    prev = lambda blk: pl.BlockSpec((None, w, kw), lambda bi, i: (bi, jnp.maximum(i * per - 1, 0), blk))
    cur = lambda blk: pl.BlockSpec((None, tq, kw), lambda bi, i: (bi, i, blk))
    nxt = lambda blk: pl.BlockSpec((None, w, kw), lambda bi, i: (bi, jnp.minimum((i + 1) * per, nblk - 1), blk))
    return pl.pallas_call(
        functools.partial(_win_attn_body, seq=s),
        grid=(b, s // tq),
        in_specs=[
            pl.BlockSpec(memory_space=pltpu.SMEM),
            pl.BlockSpec((None, tq, qw), lambda bi, i: (bi, i, qb)),
            prev(kb), cur(kb), nxt(kb),
            prev(vb), cur(vb), nxt(vb),
        ],
        out_specs=pl.BlockSpec((None, tq, qw), lambda bi, i: (bi, i, 0)),
        out_shape=jax.ShapeDtypeStruct((b, s, qw), BF16),
        compiler_params=pltpu.CompilerParams(dimension_semantics=("parallel", "parallel")),
        name="window_attention",
    )(sink, proj3, proj3, proj3, proj3, proj3, proj3, proj3)


def _mem_attn_body(q_ref, mk_ref, mv_ref, o_ref):
    xd = q_ref.shape[1]
    s = lax.dot_general(q_ref[...], mk_ref[...], (((1,), (1,)), ((), ())), preferred_element_type=F32)
    s = s * (xd ** -0.5 * LOG2E)
    m = jnp.max(s, axis=-1, keepdims=True)
    p = jnp.exp2(s - m).astype(BF16)
    mv = jnp.concatenate([mv_ref[...], jnp.ones((mv_ref.shape[0], V7X_LANES), BF16)], axis=1)
    o = jnp.dot(p, mv, preferred_element_type=F32)
    inv = 1.0 / o[:, xd:]
    o_ref[...] = (o[:, :xd] * jnp.concatenate([inv] * (xd // V7X_LANES), axis=1)).astype(o_ref.dtype)


def _memory_attention(proj3, mkv3, *, tq, q_col0):
    b, s, _ = proj3.shape
    m = mkv3.shape[1]
    xd = mkv3.shape[2] // (2 * N_X_HEADS)
    q_blk0 = q_col0 // xd
    return pl.pallas_call(
        _mem_attn_body,
        grid=(b, s // tq, N_X_HEADS),
        in_specs=[
            pl.BlockSpec((None, tq, xd), lambda bi, i, h: (bi, i, q_blk0 + h)),
            pl.BlockSpec((None, m, xd), lambda bi, i, h: (bi, 0, h)),
            pl.BlockSpec((None, m, xd), lambda bi, i, h: (bi, 0, N_X_HEADS + h)),
        ],
        out_specs=pl.BlockSpec((None, tq, xd), lambda bi, i, h: (bi, i, h)),
        out_shape=jax.ShapeDtypeStruct((b, s, N_X_HEADS * xd), BF16),
        compiler_params=pltpu.CompilerParams(dimension_semantics=("parallel", "parallel", "parallel")),
        name="memory_attention",
    )(proj3, mkv3, mkv3)


def _merge_body(h_ref, gr_ref, y1_ref, y2_ref, *rest):
    g_refs, (bg_ref, w_ref, o_ref, ylru_ref) = rest[:2 * N_BRANCH], rest[2 * N_BRANCH:]
    nch, tl, d = y1_ref.shape
    tm = nch * tl
    half = d // 2

    for s in range(nch):
        for c in range(h_ref.shape[0]):
            cs = slice(c * V7X_LANES, (c + 1) * V7X_LANES)
            hv = h_ref[c, pl.ds(s, tl, stride=nch), :]
            gate = _gelu_tanh(gr_ref[s, :, cs].astype(F32))
            ylru_ref[s * tl:(s + 1) * tl, cs] = (hv * gate).astype(ylru_ref.dtype)

    ys = (ylru_ref[...], y1_ref[...].reshape(tm, d), y2_ref[...].reshape(tm, d))
    zs = [jnp.dot(ys[br], w_ref[br], preferred_element_type=F32) for br in range(N_BRANCH)]
    for hf in range(2):
        cols = slice(hf * half, (hf + 1) * half)
        acc = None
        for br in range(N_BRANCH):
            pre = g_refs[2 * br + hf][...].reshape(tm, half).astype(F32) + bg_ref[:, br * d + hf * half:br * d + (hf + 1) * half]
            term = _sigmoid(pre) * zs[br][:, cols]
            acc = term if acc is None else acc + term
        o_ref[:, :, cols] = acc.astype(o_ref.dtype).reshape(nch, tl, half)


def _merge(h_lru4, y_attn3, y_mem3, proj3, b_gate, w_br, *, tl, gr_col0, gl_col0):
    b, seq, d = y_attn3.shape
    nch = N_CHUNKS
    clen = seq // nch
    tm = nch * tl
    half = d // 2
    g0 = gl_col0 // half
    nlb = clen // tl
    est = (N_BRANCH * d * d * 2 + 2 * tm * d * 4 + 8 * tm * d * 2 + 4 * N_BRANCH * tm * half * 2 + tm * d * 2
           + 5 * tm * d * 4)
    chunked = lambda a: a.reshape(b, nch, clen, a.shape[-1])
    rows = lambda width, col: pl.BlockSpec((None, nch, tl, width), lambda i: (i // nlb, 0, i % nlb, col))
    proj4 = chunked(proj3)
    merged = pl.pallas_call(
        _merge_body,
        grid=(b * nlb,),
        in_specs=[
            pl.BlockSpec((None, d // V7X_LANES, tm, V7X_LANES), lambda i: (i // nlb, 0, i % nlb, 0)),
            rows(d, gr_col0 // d),
            rows(d, 0), rows(d, 0),
        ] + [rows(half, g0 + idx) for idx in range(2 * N_BRANCH)] + [
            pl.BlockSpec((1, N_BRANCH * d), lambda i: (0, 0)),
            pl.BlockSpec((N_BRANCH, d, d), lambda i: (0, 0, 0), pipeline_mode=pl.Buffered(1)),
        ],
        out_specs=rows(d, 0),
        out_shape=jax.ShapeDtypeStruct((b, nch, clen, d), BF16),
        scratch_shapes=[pltpu.VMEM((tm, d), BF16)],
        compiler_params=pltpu.CompilerParams(
            dimension_semantics=("parallel",), vmem_limit_bytes=_vmem_limit(est)),
        name="branch_merge",
    )(h_lru4, proj4, chunked(y_attn3), chunked(y_mem3), *([proj4] * (2 * N_BRANCH)),
      b_gate.reshape(1, N_BRANCH * d), w_br)
    return merged.reshape(b * seq, d)


def _out_proj_body(m_ref, w_ref, g_ref, x_ref, o_ref):
    z = jnp.dot(m_ref[...], w_ref[...], preferred_element_type=F32)
    o_ref[...] = x_ref[...] + _rms_scale(z) * g_ref[...]


def _out_proj(merged, w_out, g, x, *, tm):
    t, d = x.shape
    est = 2 * tm * d * 2 + 2 * d * d * 2 + 4 * tm * d * 4 + 3 * tm * d * 4
    return pl.pallas_call(
        _out_proj_body,
        grid=(t // tm,),
        in_specs=[
            pl.BlockSpec((tm, d), lambda i: (i, 0)),
            pl.BlockSpec((d, d), lambda i: (0, 0)),
            pl.BlockSpec((1, d), lambda i: (0, 0)),
            pl.BlockSpec((tm, d), lambda i: (i, 0)),
        ],
        out_specs=pl.BlockSpec((tm, d), lambda i: (i, 0)),
        out_shape=jax.ShapeDtypeStruct((t, d), F32),
        compiler_params=pltpu.CompilerParams(
            dimension_semantics=("parallel",), vmem_limit_bytes=_vmem_limit(est)),
        name="out_proj",
    )(merged, w_out, g.reshape(1, d), x)


def _mlp_body(x_ref, gpre_ref, wu_ref, wd_ref, gpost_ref, o_ref, h_ref):
    f = pl.program_id(1)

    @pl.when(f == 0)
    def _():
        h_ref[...] = (_rms_scale(x_ref[...]) * gpre_ref[...]).astype(h_ref.dtype)
        o_ref[...] = jnp.zeros_like(o_ref)

    u = jnp.dot(h_ref[...], wu_ref[...], preferred_element_type=F32)
    u = jnp.square(jnp.maximum(u, 0.0)).astype(BF16)
    o_ref[...] += jnp.dot(u, wd_ref[...], preferred_element_type=F32)

    @pl.when(f == pl.num_programs(1) - 1)
    def _():
        o_ref[...] = x_ref[...] + _rms_scale(o_ref[...]) * gpost_ref[...]


def _mlp(x, g_pre, w_up, w_down, g_post, *, tm, tf):
    t, d = x.shape
    ff = w_up.shape[1]
    est = 4 * tm * d * 4 + 4 * d * tf * 2 + tm * d * 2 + 2 * tm * tf * 4 + 2 * tm * d * 4
    return pl.pallas_call(
        _mlp_body,
        grid=(t // tm, ff // tf),
        in_specs=[
            pl.BlockSpec((tm, d), lambda i, f: (i, 0)),
            pl.BlockSpec((1, d), lambda i, f: (0, 0)),
            pl.BlockSpec((d, tf), lambda i, f: (0, f)),
            pl.BlockSpec((tf, d), lambda i, f: (f, 0)),
            pl.BlockSpec((1, d), lambda i, f: (0, 0)),
        ],
        out_specs=pl.BlockSpec((tm, d), lambda i, f: (i, 0)),
        out_shape=jax.ShapeDtypeStruct((t, d), F32),
        scratch_shapes=[pltpu.VMEM((tm, d), BF16)],
        compiler_params=pltpu.CompilerParams(
            dimension_semantics=("parallel", "arbitrary"), vmem_limit_bytes=_vmem_limit(est)),
        name="mlp",
    )(x, g_pre.reshape(1, d), w_up, w_down, g_post.reshape(1, d))


def _layer(x, mem, positions, norm_mix_pre, norm_mix_post, norm_mem, w_in, b_gate,
           conv_w, conv_b, wr_f, br_f, wi_f, bi_f, lam_f, wr_b, br_b, wi_b, bi_b, lam_b,
           attn_sink, w_mem_kv, w_br_lru, w_br_attn, w_br_mem, w_out,
           norm_mlp_pre, norm_mlp_post, w_up, w_down):
    b, s, d = x.shape
    t = b * s
    nmem = mem.shape[1]
    bd = d // LRU_BLOCKS
    x2 = x.reshape(t, d)
    tm, tn = 1024, 1024

    n_in = w_in.shape[1]
    col_gr = 0
    col_q = col_gr + d
    col_k = col_q + N_Q_HEADS * HEAD_DIM
    col_v = col_k + N_KV_HEADS * HEAD_DIM
    col_qm = col_v + N_KV_HEADS * HEAD_DIM
    col_gl = col_qm + d
    assert n_in == d + col_gl + N_BRANCH * d
    assert col_k % tn == 0 and (col_v - col_k) * 2 == tn

    cos, sin = _rope_tables(positions, tr=1024)
    w_in16 = w_in.astype(BF16)
    rope_tiles = [(jt, tn // HEAD_DIM) for jt in range(col_q // tn, col_k // tn)] + [(col_k // tn, N_KV_HEADS)]
    proj = _norm_matmul(x2, norm_mix_pre, w_in16, cos, sin, tm=tm, tn=tn, name="in_proj",
                        rope_tiles=rope_tiles, col0=d)
    proj3 = proj.reshape(b, s, n_in - d)
    xr4 = _norm_matmul_chunked(x, norm_mix_pre, w_in16, tl=tm // N_CHUNKS, tn=tn, ncols=d, name="in_proj_lru")

    cat = lambda a, c: 0.5 * jnp.concatenate([a, c], axis=-1)
    wf = cat(wr_f, wi_f).astype(BF16)
    wb = cat(wr_b, wi_b).astype(BF16)
    bf = cat(br_f, bi_f).reshape(LRU_BLOCKS, 1, 2 * bd)
    bb = cat(br_b, bi_b).reshape(LRU_BLOCKS, 1, 2 * bd)
    h_lru = _lru_branch(
        xr4, wf, wb, bf, bb,
        lam_f.reshape(LRU_BLOCKS, 1, bd), lam_b.reshape(LRU_BLOCKS, 1, bd),
        conv_w.reshape(CONV_WIDTH, LRU_BLOCKS, bd).transpose(1, 0, 2), conv_b.reshape(LRU_BLOCKS, 1, bd),
        tl=32, unroll=8)

    y_attn = _window_attention(proj3, attn_sink, tq=1024,q_col0=col_q, k_col0=col_k, v_col0=col_v)

    mkv = _norm_matmul(mem.reshape(b * nmem, d), norm_mem, w_mem_kv.astype(BF16), cos, sin,
                       tm=b * nmem, tn=tn, name="mem_kv")
    y_mem = _memory_attention(proj3, mkv.reshape(b, nmem, 2 * d), tq=2048, q_col0=col_qm)

    w_br = jnp.stack([w_br_lru, w_br_attn, w_br_mem]).astype(BF16)
    merged = _merge(h_lru, y_attn, y_mem, proj3, b_gate, w_br, tl=32, gr_col0=col_gr, gl_col0=col_gl)
    x1 = _out_proj(merged, w_out.astype(BF16), norm_mix_post, x2, tm=512)

    out = _mlp(x1, norm_mlp_pre, w_up.astype(BF16), w_down.astype(BF16), norm_mlp_post, tm=1024, tf=512)
    return out.reshape(b, s, d)


def kernel(x, mem, positions, norm_mix_pre, norm_mix_post, norm_mem, w_in, b_gate, conv_w, conv_b, wr_f, br_f, wi_f, bi_f, lam_f, wr_b, br_b, wi_b, bi_b, lam_b, attn_sink, w_mem_kv, w_br_lru, w_br_attn, w_br_mem, w_out, norm_mlp_pre, norm_mlp_post, w_up, w_down):
    depth = w_in.shape[0]
    for l in range(depth):
        x = _layer(x, mem, positions, norm_mix_pre[l], norm_mix_post[l], norm_mem[l],
                   w_in[l], b_gate[l], conv_w[l], conv_b[l],
                   wr_f[l], br_f[l], wi_f[l], bi_f[l], lam_f[l],
                   wr_b[l], br_b[l], wi_b[l], bi_b[l], lam_b[l],
                   attn_sink[l], w_mem_kv[l], w_br_lru[l], w_br_attn[l], w_br_mem[l],
                   w_out[l], norm_mlp_pre[l], norm_mlp_post[l], w_up[l], w_down[l])
    return x
```

```python
import functools
import math

import jax
import jax.numpy as jnp
from jax import lax
from jax.experimental import pallas as pl
from jax.experimental.pallas import tpu as pltpu

LRU_BLOCKS = 16
LRU_C = 8.0
HEAD_DIM = 128
N_Q_HEADS = 16
N_KV_HEADS = 4
Q_GROUP = N_Q_HEADS // N_KV_HEADS
WINDOW = 128
N_X_HEADS = 4
N_BRANCH = 3
ROPE_THETA = 10000.0
EPS = 1e-6
CONV_WIDTH = 4

V7X_LANES = 128
V7X_SUBLANES = 8
V7X_BF16_SUBLANES = 16
V7X_VMEM_BYTES = 64 * 1024 * 1024

N_CHUNKS = V7X_SUBLANES

F32 = jnp.float32
BF16 = jnp.bfloat16
MASK_VALUE = -1e30
LOG2E = math.log2(math.e)


def _vmem_limit(nbytes):
    return int(min(nbytes, V7X_VMEM_BYTES - 4 * 1024 * 1024))


def _sigmoid(x):
    return 0.5 * jnp.tanh(0.5 * x) + 0.5


def _rms_scale(x):
    var = jnp.mean(x * x, axis=-1, keepdims=True)
    return x * lax.rsqrt(var + EPS)


def _gelu_tanh(x):
    return 0.5 * x * (1.0 + jnp.tanh(math.sqrt(2.0 / math.pi) * (x + 0.044715 * (x * x * x))))


def _norm_matmul_body(x_ref, g_ref, w_ref, cos_ref, sin_ref, o_ref, h_ref, *, rope_tiles):
    j = pl.program_id(1)

    @pl.when(j == 0)
    def _():
        h_ref[...] = (_rms_scale(x_ref[...]) * g_ref[...]).astype(h_ref.dtype)

    def compute(n_rope):
        acc = jnp.dot(h_ref[...], w_ref[...], preferred_element_type=F32)
        if n_rope == 0:
            o_ref[...] = acc.astype(o_ref.dtype)
            return
        for h in range(acc.shape[1] // HEAD_DIM):
            sl = slice(h * HEAD_DIM, (h + 1) * HEAD_DIM)
            t = acc[:, sl]
            if h < n_rope:
                t = t * cos_ref[...] + pltpu.roll(t, HEAD_DIM // 2, 1) * sin_ref[...]
            o_ref[:, sl] = t.astype(o_ref.dtype)

    plain = None
    for tile, n_rope in rope_tiles:
        pl.when(j == tile)(functools.partial(compute, n_rope))
        plain = (j != tile) if plain is None else plain & (j != tile)
    if plain is None:
        compute(0)
    else:
        pl.when(plain)(functools.partial(compute, 0))


def _norm_matmul(x, g, w, cos, sin, *, tm, tn, name, rope_tiles=(), col0=0, ncols=None):
    m, d = x.shape
    n = w.shape[1] - col0 if ncols is None else ncols
    j0 = col0 // tn
    est = 2 * tm * d * 4 + 2 * d * tn * 2 + 2 * tm * tn * 2 + tm * d * 2 + 3 * tm * tn * 4 + tm * d * 4
    out_shape = jax.ShapeDtypeStruct((m, n), BF16)
    out_spec = pl.BlockSpec((tm, tn), lambda i, j: (i, j))
    return pl.pallas_call(
        functools.partial(_norm_matmul_body, rope_tiles=tuple(rope_tiles)),
        grid=(m // tm, n // tn),
        in_specs=[
            pl.BlockSpec((tm, d), lambda i, j: (i, 0)),
            pl.BlockSpec((1, d), lambda i, j: (0, 0)),
            pl.BlockSpec((d, tn), lambda i, j: (0, j0 + j)),
            pl.BlockSpec((tm, HEAD_DIM), lambda i, j: (i, 0)),
            pl.BlockSpec((tm, HEAD_DIM), lambda i, j: (i, 0)),
        ],
        out_specs=out_spec,
        out_shape=out_shape,
        scratch_shapes=[pltpu.VMEM((tm, d), BF16)],
        compiler_params=pltpu.CompilerParams(
            dimension_semantics=("parallel", "arbitrary"), vmem_limit_bytes=_vmem_limit(est)),
        name=name,
    )(x, g.reshape(1, d), w, cos, sin)


def _norm_matmul_chunked_body(x_ref, g_ref, w_ref, o_ref, h_ref):
    nch, tl, d = x_ref.shape

    @pl.when(pl.program_id(1) == 0)
    def _():
        x = x_ref[...].reshape(nch * tl, d)
        h_ref[...] = (_rms_scale(x) * g_ref[...]).astype(h_ref.dtype)

    acc = jnp.dot(h_ref[...], w_ref[...], preferred_element_type=F32)
    for s in range(nch):
        for c in range(o_ref.shape[0]):
            o_ref[c, pl.ds(s, tl, stride=nch), :] = acc[s * tl:(s + 1) * tl, c * V7X_LANES:(c + 1) * V7X_LANES]


def _norm_matmul_chunked(x3, g, w, *, tl, tn, ncols, name):
    b, seq, d = x3.shape
    n = ncols
    nch = N_CHUNKS
    clen = seq // nch
    tm = nch * tl
    est = 2 * tm * d * 4 + 2 * d * tn * 2 + 2 * tm * tn * 4 + tm * d * 2 + 2 * tm * tn * 4 + tm * d * 4
    return pl.pallas_call(
        _norm_matmul_chunked_body,
        grid=(b * (clen // tl), n // tn),
        in_specs=[
            pl.BlockSpec((None, nch, tl, d), lambda i, j: (i // (clen // tl), 0, i % (clen // tl), 0)),
            pl.BlockSpec((1, d), lambda i, j: (0, 0)),
            pl.BlockSpec((d, tn), lambda i, j: (0, j)),
        ],
        out_specs=pl.BlockSpec((None, tn // V7X_LANES, tm, V7X_LANES),
                               lambda i, j: (i // (clen // tl), j, i % (clen // tl), 0)),
        out_shape=jax.ShapeDtypeStruct((b, n // V7X_LANES, seq, V7X_LANES), F32),
        scratch_shapes=[pltpu.VMEM((tm, d), BF16)],
        compiler_params=pltpu.CompilerParams(
            dimension_semantics=("parallel", "arbitrary"), vmem_limit_bytes=_vmem_limit(est)),
        name=name,
    )(x3.reshape(b, nch, clen, d), g.reshape(1, d), w)


def _rope_table_body(pos_ref, freq_ref, sign_ref, cos_ref, sin_ref):
    ang = pos_ref[...].astype(F32) * freq_ref[...]
    cos_ref[...] = jnp.cos(ang)
    sin_ref[...] = jnp.sin(ang) * sign_ref[...]


def _rope_tables(positions, *, tr):
    t = positions.size
    half = HEAD_DIM // 2
    freqs = ROPE_THETA ** (-jnp.arange(half, dtype=F32) / half)
    freq2 = jnp.concatenate([freqs, freqs]).reshape(1, HEAD_DIM)
    sign = jnp.concatenate([-jnp.ones((half,), F32), jnp.ones((half,), F32)]).reshape(1, HEAD_DIM)
    tab = jax.ShapeDtypeStruct((t, HEAD_DIM), F32)
    return pl.pallas_call(
        _rope_table_body,
        grid=(t // tr,),
        in_specs=[
            pl.BlockSpec((tr, 1), lambda i: (i, 0)),
            pl.BlockSpec((1, HEAD_DIM), lambda i: (0, 0)),
            pl.BlockSpec((1, HEAD_DIM), lambda i: (0, 0)),
        ],
        out_specs=[pl.BlockSpec((tr, HEAD_DIM), lambda i: (i, 0))] * 2,
        out_shape=[tab, tab],
        compiler_params=pltpu.CompilerParams(dimension_semantics=("parallel",)),
        name="rope_tables",
    )(positions.reshape(t, 1), freq2, sign)


def _sublane_scan(av, uv, masks, reverse):
    for d, m in zip((1, 2, 4), masks):
        sh = V7X_SUBLANES - d if reverse else d
        us = jnp.where(m, 0.0, pltpu.roll(uv, sh, 0))
        as_ = jnp.where(m, 1.0, pltpu.roll(av, sh, 0))
        uv = uv + av * us
        av = av * as_
    return uv


def _lru_body(xr_ref, wf_ref, wb_ref, bf_ref, bb_ref, lamf_ref, lamb_ref, cw_ref, cb_ref,
              o_ref, p_ref, xc_ref, *, tl, unroll):
    rows = xr_ref.shape[0]
    sub = V7X_SUBLANES
    tr = tl * sub
    n = rows // tr
    cw = cw_ref[...]
    cb = cb_ref[...]
    row = lax.broadcasted_iota(jnp.int32, (sub, V7X_LANES), 0)
    masks_f = tuple(row < d for d in (1, 2, 4))
    masks_b = tuple(row >= sub - d for d in (1, 2, 4))

    def neg_c_softplus(lam_ref):
        z = -lam_ref[...]
        sp = jnp.maximum(z, 0.0) + jnp.log1p(jnp.exp(-jnp.abs(z)))
        return (0.5 * LRU_C) * sp

    def load(r0, nrows):
        return xr_ref[pl.ds(r0, nrows), :]

    def conv_tile(j):
        r0 = pl.multiple_of(j * tr, tr)
        cur = load(r0, tr)
        before = load(pl.multiple_of(jnp.maximum(r0 - sub, 0), sub), sub)
        after = load(pl.multiple_of(jnp.minimum(r0 + tr, rows - 2 * sub), sub), 2 * sub)
        last = load(rows - sub, sub)
        first = load(0, 2 * sub)
        wrap_m1 = jnp.where(masks_f[0], 0.0, pltpu.roll(last, 1, 0))
        wrap_p1 = jnp.where(masks_b[0], 0.0, pltpu.roll(first[:sub], sub - 1, 0))
        wrap_p2 = jnp.where(masks_b[0], 0.0, pltpu.roll(first[sub:], sub - 1, 0))
        xm1 = jnp.where(j > 0, before, wrap_m1)
        xp1 = jnp.where(j < n - 1, after[:sub], wrap_p1)
        xp2 = jnp.where(j < n - 1, after[sub:], wrap_p2)
        ext = jnp.concatenate([xm1, cur, xp1, xp2], axis=0)
        xc = (cw[0:1] * ext[0:tr] + cw[1:2] * cur
              + cw[2:3] * ext[2 * sub:2 * sub + tr] + cw[3:4] * ext[3 * sub:3 * sub + tr])
        return xc + cb, r0

    def gates(xc, w_ref, b_ref, c):
        g = jnp.tanh(jnp.dot(xc.astype(BF16), w_ref[...], preferred_element_type=F32) + b_ref[...])
        i = 0.5 * g[:, V7X_LANES:] + 0.5
        nla = c * g[:, :V7X_LANES] + c
        a = jnp.exp(-nla)
        one_minus_a2 = jnp.tanh(nla) * (a * a + 1.0)
        root = jnp.where(one_minus_a2 > 0.0, one_minus_a2 * lax.rsqrt(one_minus_a2), 0.0)
        u = root * (i * xc)
        return a, u

    def local_scan(a, u, h, p, reverse):
        hs, ps = [None] * tl, [None] * tl
        for q in (range(tl - 1, -1, -1) if reverse else range(tl)):
            av = a[q * sub:(q + 1) * sub]
            h = av * h + u[q * sub:(q + 1) * sub]
            p = p * av
            hs[q], ps[q] = h, p
        return jnp.concatenate(hs, axis=0), jnp.concatenate(ps, axis=0), h, p

    def chunk_carries(hend, ptot, masks, reverse):
        ends = _sublane_scan(ptot, hend, masks, reverse)
        return jnp.where(masks[0], 0.0, pltpu.roll(ends, sub - 1 if reverse else 1, 0))

    zeros = jnp.zeros((sub, V7X_LANES), F32)
    ones = jnp.ones((sub, V7X_LANES), F32)
    c_f = neg_c_softplus(lamf_ref)
    c_b = neg_c_softplus(lamb_ref)

    def fwd_step(j, carry):
        xc, r0 = conv_tile(j)
        a, u = gates(xc, wf_ref, bf_ref, c_f)
        hs, ps, h, p = local_scan(a, u, carry[0], carry[1], reverse=False)
        o_ref[pl.ds(r0, tr), :] = hs
        p_ref[pl.ds(r0, tr), :] = ps
        xc_ref[pl.ds(r0, tr), :] = xc
        return h, p

    hend, ptot = lax.fori_loop(0, n, fwd_step, (zeros, ones), unroll=unroll)
    carry_f = chunk_carries(hend, ptot, masks_f, reverse=False)

    def bwd_step(jj, carry):
        sl = pl.ds(pl.multiple_of((n - 1 - jj) * tr, tr), tr)
        a, u = gates(xc_ref[sl, :], wb_ref, bb_ref, c_b)
        hs, ps, h, p = local_scan(a, u, carry[0], carry[1], reverse=True)
        o_ref[sl, :] = o_ref[sl, :] + p_ref[sl, :] * jnp.concatenate([carry_f] * tl, axis=0) + hs
        p_ref[sl, :] = ps
        return h, p

    hend, ptot = lax.fori_loop(0, n, bwd_step, (zeros, ones), unroll=unroll)
    carry_b = chunk_carries(hend, ptot, masks_b, reverse=True)

    def out_step(j, _):
        sl = pl.ds(pl.multiple_of(j * tr, tr), tr)
        o_ref[sl, :] = o_ref[sl, :] + p_ref[sl, :] * jnp.concatenate([carry_b] * tl, axis=0)
        return 0

    lax.fori_loop(0, n, out_step, 0, unroll=unroll)


def _lru_branch(xr4, wf, wb, bf, bb, lam_f, lam_b, conv_w, conv_b, *, tl, unroll):
    b, c, rows, bd = xr4.shape
    est = 4 * rows * bd * 4 + 2 * rows * bd * 4 + 8 * 1024 * 1024
    blk = lambda shape: pl.BlockSpec((None,) + shape, lambda bi, ci: (ci,) + (0,) * len(shape))
    slab = pl.BlockSpec((None, None, rows, bd), lambda bi, ci: (bi, ci, 0, 0))
    return pl.pallas_call(
        functools.partial(_lru_body, tl=tl, unroll=unroll),
        grid=(b, c),
        in_specs=[
            slab,
            blk((bd, 2 * bd)), blk((bd, 2 * bd)),
            blk((1, 2 * bd)), blk((1, 2 * bd)),
            blk((1, bd)), blk((1, bd)),
            blk((CONV_WIDTH, bd)), blk((1, bd)),
        ],
        out_specs=slab,
        out_shape=jax.ShapeDtypeStruct(xr4.shape, F32),
        scratch_shapes=[pltpu.VMEM((rows, bd), F32)] * 2,
        compiler_params=pltpu.CompilerParams(
            dimension_semantics=("parallel", "parallel"), vmem_limit_bytes=_vmem_limit(est)),
        name="rglru",
    )(xr4, wf, wb, bf, bb, lam_f, lam_b, conv_w, conv_b)


def _win_attn_body(sink_ref, q_ref, kp_ref, kc_ref, kn_ref, vp_ref, vc_ref, vn_ref, o_ref, *, seq):
    i = pl.program_id(1)
    tq = q_ref.shape[0]
    w = WINDOW
    nqb = tq // w
    nk = 3 * w
    keyi = lax.broadcasted_iota(jnp.int32, (nk, w), 0)
    qryi = lax.broadcasted_iota(jnp.int32, (nk, w), 1)
    band = jnp.where((keyi >= qryi) & (keyi <= qryi + 2 * w), 0.0, MASK_VALUE)
    edge = {}
    for qb in {0, nqb - 1}:
        kpos = keyi + (i * tq + (qb - 1) * w)
        edge[qb] = band + jnp.where((kpos >= 0) & (kpos < seq), 0.0, MASK_VALUE)
    scale = HEAD_DIM ** -0.5 * LOG2E
    for g in range(N_KV_HEADS):
        gs = slice(g * HEAD_DIM, (g + 1) * HEAD_DIM)
        k = jnp.concatenate([kp_ref[:, gs], kc_ref[:, gs], kn_ref[:, gs]], axis=0)
        v = jnp.concatenate([vp_ref[:, gs], vc_ref[:, gs], vn_ref[:, gs]], axis=0)
        sink = jnp.concatenate(
            [jnp.full((1, w), sink_ref[g * Q_GROUP + r] * LOG2E, F32) for r in range(Q_GROUP)], axis=1)
        for qb in range(nqb):
            rows = slice(qb * w, (qb + 1) * w)
            heads = [slice((g * Q_GROUP + r) * HEAD_DIM, (g * Q_GROUP + r + 1) * HEAD_DIM) for r in range(Q_GROUP)]
            q = jnp.concatenate([q_ref[rows, hs] for hs in heads], axis=0)
            kb = k[qb * w:qb * w + nk]
            vb = v[qb * w:qb * w + nk]
            mask = edge.get(qb, band)
            st = lax.dot_general(kb, q, (((1,), (1,)), ((), ())), preferred_element_type=F32)
            st = st * scale + jnp.concatenate([mask] * Q_GROUP, axis=1)
            m = jnp.maximum(jnp.max(st, axis=0, keepdims=True), sink)
            p = jnp.exp2(st - m)
            denom = jnp.sum(p, axis=0, keepdims=True) + jnp.exp2(sink - m)
            ot = lax.dot_general(vb, p.astype(BF16), (((0,), (0,)), ((), ())), preferred_element_type=F32)
            ot = ot / denom
            for r, hs in enumerate(heads):
                o_ref[rows, hs] = ot[:, r * w:(r + 1) * w].T.astype(o_ref.dtype)


def _window_attention(proj3, sink, *, tq, q_col0, k_col0, v_col0):
    b, s, _ = proj3.shape
    w = WINDOW
    per = tq // w
    nblk = s // w
    qw, kw = N_Q_HEADS * HEAD_DIM, N_KV_HEADS * HEAD_DIM
    qb, kb, vb = q_col0 // qw, k_col0 // kw, v_col0 // kw
    prev = lambda blk: pl.BlockSpec((None, w, kw), lambda bi, i: (bi, jnp.maximum(i * per - 1, 0), blk))
    cur = lambda blk: pl.BlockSpec((None, tq, kw), lambda bi, i: (bi, i, blk))
    nxt = lambda blk: pl.BlockSpec((None, w, kw), lambda bi, i: (bi, jnp.minimum((i + 1) * per, nblk - 1), blk))
    return pl.pallas_call(
        functools.partial(_win_attn_body, seq=s),
        grid=(b, s // tq),
        in_specs=[
            pl.BlockSpec(memory_space=pltpu.SMEM),
            pl.BlockSpec((None, tq, qw), lambda bi, i: (bi, i, qb)),
            prev(kb), cur(kb), nxt(kb),
            prev(vb), cur(vb), nxt(vb),
        ],
        out_specs=pl.BlockSpec((None, tq, qw), lambda bi, i: (bi, i, 0)),
        out_shape=jax.ShapeDtypeStruct((b, s, qw), BF16),
        compiler_params=pltpu.CompilerParams(dimension_semantics=("parallel", "parallel")),
        name="window_attention",
    )(sink, proj3, proj3, proj3, proj3, proj3, proj3, proj3)


def _mem_attn_body(q_ref, mk_ref, mv_ref, o_ref):
    xd = q_ref.shape[1]
    s = lax.dot_general(q_ref[...], mk_ref[...], (((1,), (1,)), ((), ())), preferred_element_type=F32)
    s = s * (xd ** -0.5 * LOG2E)
    m = jnp.max(s, axis=-1, keepdims=True)
    p = jnp.exp2(s - m).astype(BF16)
    mv = jnp.concatenate([mv_ref[...], jnp.ones((mv_ref.shape[0], V7X_LANES), BF16)], axis=1)
    o = jnp.dot(p, mv, preferred_element_type=F32)
    inv = 1.0 / o[:, xd:]
    o_ref[...] = (o[:, :xd] * jnp.concatenate([inv] * (xd // V7X_LANES), axis=1)).astype(o_ref.dtype)


def _memory_attention(proj3, mkv3, *, tq, q_col0):
    b, s, _ = proj3.shape
    m = mkv3.shape[1]
    xd = mkv3.shape[2] // (2 * N_X_HEADS)
    q_blk0 = q_col0 // xd
    return pl.pallas_call(
        _mem_attn_body,
        grid=(b, s // tq, N_X_HEADS),
        in_specs=[
            pl.BlockSpec((None, tq, xd), lambda bi, i, h: (bi, i, q_blk0 + h)),
            pl.BlockSpec((None, m, xd), lambda bi, i, h: (bi, 0, h)),
            pl.BlockSpec((None, m, xd), lambda bi, i, h: (bi, 0, N_X_HEADS + h)),
        ],
        out_specs=pl.BlockSpec((None, tq, xd), lambda bi, i, h: (bi, i, h)),
        out_shape=jax.ShapeDtypeStruct((b, s, N_X_HEADS * xd), BF16),
        compiler_params=pltpu.CompilerParams(dimension_semantics=("parallel", "parallel", "parallel")),
        name="memory_attention",
    )(proj3, mkv3, mkv3)


def _merge_body(h_ref, gr_ref, y1_ref, y2_ref, *rest):
    g_refs, (bg_ref, w_ref, o_ref, ylru_ref) = rest[:2 * N_BRANCH], rest[2 * N_BRANCH:]
    nch, tl, d = y1_ref.shape
    tm = nch * tl
    half = d // 2

    for s in range(nch):
        for c in range(h_ref.shape[0]):
            cs = slice(c * V7X_LANES, (c + 1) * V7X_LANES)
            hv = h_ref[c, pl.ds(s, tl, stride=nch), :]
            gate = _gelu_tanh(gr_ref[s, :, cs].astype(F32))
            ylru_ref[s * tl:(s + 1) * tl, cs] = (hv * gate).astype(ylru_ref.dtype)

    ys = (ylru_ref[...], y1_ref[...].reshape(tm, d), y2_ref[...].reshape(tm, d))
    zs = [jnp.dot(ys[br], w_ref[br], preferred_element_type=F32) for br in range(N_BRANCH)]
    for hf in range(2):
        cols = slice(hf * half, (hf + 1) * half)
        acc = None
        for br in range(N_BRANCH):
            pre = g_refs[2 * br + hf][...].reshape(tm, half).astype(F32) + bg_ref[:, br * d + hf * half:br * d + (hf + 1) * half]
            term = _sigmoid(pre) * zs[br][:, cols]
            acc = term if acc is None else acc + term
        o_ref[:, :, cols] = acc.astype(o_ref.dtype).reshape(nch, tl, half)


def _merge(h_lru4, y_attn3, y_mem3, proj3, b_gate, w_br, *, tl, gr_col0, gl_col0):
    b, seq, d = y_attn3.shape
    nch = N_CHUNKS
    clen = seq // nch
    tm = nch * tl
    half = d // 2
    g0 = gl_col0 // half
    nlb = clen // tl
    est = (N_BRANCH * d * d * 2 + 2 * tm * d * 4 + 8 * tm * d * 2 + 4 * N_BRANCH * tm * half * 2 + tm * d * 2
           + 5 * tm * d * 4)
    chunked = lambda a: a.reshape(b, nch, clen, a.shape[-1])
    rows = lambda width, col: pl.BlockSpec((None, nch, tl, width), lambda i: (i // nlb, 0, i % nlb, col))
    proj4 = chunked(proj3)
    merged = pl.pallas_call(
        _merge_body,
        grid=(b * nlb,),
        in_specs=[
            pl.BlockSpec((None, d // V7X_LANES, tm, V7X_LANES), lambda i: (i // nlb, 0, i % nlb, 0)),
            rows(d, gr_col0 // d),
            rows(d, 0), rows(d, 0),
        ] + [rows(half, g0 + idx) for idx in range(2 * N_BRANCH)] + [
            pl.BlockSpec((1, N_BRANCH * d), lambda i: (0, 0)),
            pl.BlockSpec((N_BRANCH, d, d), lambda i: (0, 0, 0), pipeline_mode=pl.Buffered(1)),
        ],
        out_specs=rows(d, 0),
        out_shape=jax.ShapeDtypeStruct((b, nch, clen, d), BF16),
        scratch_shapes=[pltpu.VMEM((tm, d), BF16)],
        compiler_params=pltpu.CompilerParams(
            dimension_semantics=("parallel",), vmem_limit_bytes=_vmem_limit(est)),
        name="branch_merge",
    )(h_lru4, proj4, chunked(y_attn3), chunked(y_mem3), *([proj4] * (2 * N_BRANCH)),
      b_gate.reshape(1, N_BRANCH * d), w_br)
    return merged.reshape(b * seq, d)


def _out_proj_body(m_ref, w_ref, g_ref, x_ref, o_ref):
    z = jnp.dot(m_ref[...], w_ref[...], preferred_element_type=F32)
    o_ref[...] = x_ref[...] + _rms_scale(z) * g_ref[...]


def _out_proj(merged, w_out, g, x, *, tm):
    t, d = x.shape
    est = 2 * tm * d * 2 + 2 * d * d * 2 + 4 * tm * d * 4 + 3 * tm * d * 4
    return pl.pallas_call(
        _out_proj_body,
        grid=(t // tm,),
        in_specs=[
            pl.BlockSpec((tm, d), lambda i: (i, 0)),
            pl.BlockSpec((d, d), lambda i: (0, 0)),
            pl.BlockSpec((1, d), lambda i: (0, 0)),
            pl.BlockSpec((tm, d), lambda i: (i, 0)),
        ],
        out_specs=pl.BlockSpec((tm, d), lambda i: (i, 0)),
        out_shape=jax.ShapeDtypeStruct((t, d), F32),
        compiler_params=pltpu.CompilerParams(
            dimension_semantics=("parallel",), vmem_limit_bytes=_vmem_limit(est)),
        name="out_proj",
    )(merged, w_out, g.reshape(1, d), x)


def _mlp_body(x_ref, gpre_ref, wu_ref, wd_ref, gpost_ref, o_ref, h_ref):
    f = pl.program_id(1)

    @pl.when(f == 0)
    def _():
        h_ref[...] = (_rms_scale(x_ref[...]) * gpre_ref[...]).astype(h_ref.dtype)
        o_ref[...] = jnp.zeros_like(o_ref)

    u = jnp.dot(h_ref[...], wu_ref[...], preferred_element_type=F32)
    u = jnp.square(jnp.maximum(u, 0.0)).astype(BF16)
    o_ref[...] += jnp.dot(u, wd_ref[...], preferred_element_type=F32)

    @pl.when(f == pl.num_programs(1) - 1)
    def _():
        o_ref[...] = x_ref[...] + _rms_scale(o_ref[...]) * gpost_ref[...]


def _mlp(x, g_pre, w_up, w_down, g_post, *, tm, tf):
    t, d = x.shape
    ff = w_up.shape[1]
    est = 4 * tm * d * 4 + 4 * d * tf * 2 + tm * d * 2 + 2 * tm * tf * 4 + 2 * tm * d * 4
    return pl.pallas_call(
        _mlp_body,
        grid=(t // tm, ff // tf),
        in_specs=[
            pl.BlockSpec((tm, d), lambda i, f: (i, 0), pipeline_mode=pl.Buffered(1)),
            pl.BlockSpec((1, d), lambda i, f: (0, 0)),
            pl.BlockSpec((d, tf), lambda i, f: (0, f)),
            pl.BlockSpec((tf, d), lambda i, f: (f, 0)),
            pl.BlockSpec((1, d), lambda i, f: (0, 0)),
        ],
        out_specs=pl.BlockSpec((tm, d), lambda i, f: (i, 0)),
        out_shape=jax.ShapeDtypeStruct((t, d), F32),
        scratch_shapes=[pltpu.VMEM((tm, d), BF16)],
        compiler_params=pltpu.CompilerParams(
            dimension_semantics=("parallel", "arbitrary"), vmem_limit_bytes=_vmem_limit(est)),
        name="mlp",
    )(x, g_pre.reshape(1, d), w_up, w_down, g_post.reshape(1, d))


def _layer(x, mem, positions, norm_mix_pre, norm_mix_post, norm_mem, w_in, b_gate,
           conv_w, conv_b, wr_f, br_f, wi_f, bi_f, lam_f, wr_b, br_b, wi_b, bi_b, lam_b,
           attn_sink, w_mem_kv, w_br_lru, w_br_attn, w_br_mem, w_out,
           norm_mlp_pre, norm_mlp_post, w_up, w_down):
    b, s, d = x.shape
    t = b * s
    nmem = mem.shape[1]
    bd = d // LRU_BLOCKS
    x2 = x.reshape(t, d)
    tm, tn = 1024, 1024

    n_in = w_in.shape[1]
    col_gr = 0
    col_q = col_gr + d
    col_k = col_q + N_Q_HEADS * HEAD_DIM
    col_v = col_k + N_KV_HEADS * HEAD_DIM
    col_qm = col_v + N_KV_HEADS * HEAD_DIM
    col_gl = col_qm + d
    assert n_in == d + col_gl + N_BRANCH * d
    assert col_k % tn == 0 and (col_v - col_k) * 2 == tn

    cos, sin = _rope_tables(positions, tr=1024)
    w_in16 = w_in.astype(BF16)
    rope_tiles = [(jt, tn // HEAD_DIM) for jt in range(col_q // tn, col_k // tn)] + [(col_k // tn, N_KV_HEADS)]
    proj = _norm_matmul(x2, norm_mix_pre, w_in16, cos, sin, tm=tm, tn=tn, name="in_proj",
                        rope_tiles=rope_tiles, col0=d)
    proj3 = proj.reshape(b, s, n_in - d)
    xr4 = _norm_matmul_chunked(x, norm_mix_pre, w_in16, tl=tm // N_CHUNKS, tn=tn, ncols=d, name="in_proj_lru")

    cat = lambda a, c: 0.5 * jnp.concatenate([a, c], axis=-1)
    wf = cat(wr_f, wi_f).astype(BF16)
    wb = cat(wr_b, wi_b).astype(BF16)
    bf = cat(br_f, bi_f).reshape(LRU_BLOCKS, 1, 2 * bd)
    bb = cat(br_b, bi_b).reshape(LRU_BLOCKS, 1, 2 * bd)
    h_lru = _lru_branch(
        xr4, wf, wb, bf, bb,
        lam_f.reshape(LRU_BLOCKS, 1, bd), lam_b.reshape(LRU_BLOCKS, 1, bd),
        conv_w.reshape(CONV_WIDTH, LRU_BLOCKS, bd).transpose(1, 0, 2), conv_b.reshape(LRU_BLOCKS, 1, bd),
        tl=32, unroll=8)

    y_attn = _window_attention(proj3, attn_sink, tq=1024,q_col0=col_q, k_col0=col_k, v_col0=col_v)

    mkv = _norm_matmul(mem.reshape(b * nmem, d), norm_mem, w_mem_kv.astype(BF16), cos, sin,
                       tm=b * nmem, tn=tn, name="mem_kv")
    y_mem = _memory_attention(proj3, mkv.reshape(b, nmem, 2 * d), tq=2048, q_col0=col_qm)

    w_br = jnp.stack([w_br_lru, w_br_attn, w_br_mem]).astype(BF16)
    merged = _merge(h_lru, y_attn, y_mem, proj3, b_gate, w_br, tl=32, gr_col0=col_gr, gl_col0=col_gl)
    x1 = _out_proj(merged, w_out.astype(BF16), norm_mix_post, x2, tm=512)

    out = _mlp(x1, norm_mlp_pre, w_up.astype(BF16), w_down.astype(BF16), norm_mlp_post, tm=1024, tf=1024)
    return out.reshape(b, s, d)


def kernel(x, mem, positions, norm_mix_pre, norm_mix_post, norm_mem, w_in, b_gate, conv_w, conv_b, wr_f, br_f, wi_f, bi_f, lam_f, wr_b, br_b, wi_b, bi_b, lam_b, attn_sink, w_mem_kv, w_br_lru, w_br_attn, w_br_mem, w_out, norm_mlp_pre, norm_mlp_post, w_up, w_down):
    depth = w_in.shape[0]
    for l in range(depth):
        x = _layer(x, mem, positions, norm_mix_pre[l], norm_mix_post[l], norm_mem[l],
                   w_in[l], b_gate[l], conv_w[l], conv_b[l],
                   wr_f[l], br_f[l], wi_f[l], bi_f[l], lam_f[l],
                   wr_b[l], br_b[l], wi_b[l], bi_b[l], lam_b[l],
                   attn_sink[l], w_mem_kv[l], w_br_lru[l], w_br_attn[l], w_br_mem[l],
                   w_out[l], norm_mlp_pre[l], norm_mlp_post[l], w_up[l], w_down[l])
    return x
```
